```python
import math
import jax, jax.numpy as jnp
from jax import lax
import numpy as np

D_MODEL = 1024
BATCH = 16
SEQ = 2048
DEPTH = 1

CHUNK = 64
SB_BLOCK = 128
EPS = 1e-6
DN_HEADS = 4
DN_DK = 128
DN_DV = 128
DN_CONV = 4
SB_HEADS = 8
SB_DH = 64
DN_WIDTH = DN_HEADS * DN_DV
SB_WIDTH = SB_HEADS * SB_DH
N_BRANCH = 2
A_QKV1 = 2 * DN_HEADS * DN_DK + DN_WIDTH
A_Z1 = A_QKV1 + DN_WIDTH
A_B1 = A_Z1 + DN_HEADS
A_A1 = A_B1 + DN_HEADS
B_QKV1 = A_A1 + 3 * SB_WIDTH
B_Z1 = B_QKV1 + SB_WIDTH
G1 = B_Z1 + N_BRANCH * D_MODEL
PROJ_WIDTH = G1

kernel_name = 'hybrid_deltanet_stickbreaking_gated_merge'


def _rms(x, gain):
    xf = x.astype(jnp.float32)
    return xf * lax.rsqrt(jnp.mean(xf * xf, axis=-1, keepdims=True) + EPS) * gain.astype(jnp.float32)


def _l2n(x):
    return x * lax.rsqrt(jnp.sum(x * x, axis=-1, keepdims=True) + EPS)


def _causal_dwconv(u, w):
    k_len, c = w.shape
    return lax.conv_general_dilated(u, w[:, None, :].astype(u.dtype), window_strides=(1,),
                                    padding=[(k_len - 1, 0)],
                                    dimension_numbers=('NWC', 'WIO', 'NWC'),
                                    feature_group_count=c)


def _gated_delta_rule(q, k, v, g, beta):
    b, s, h, dk = q.shape
    dv = v.shape[-1]
    n = s // CHUNK

    def blocks(t):
        return jnp.moveaxis(t.reshape(b, n, CHUNK, h, -1), 3, 1)

    q = blocks(q) * dk ** -0.5
    k = blocks(k)
    v = blocks(v)
    g = jnp.moveaxis(g.reshape(b, n, CHUNK, h), 3, 1)
    beta = jnp.moveaxis(beta.reshape(b, n, CHUNK, h), 3, 1)
    gc = jnp.cumsum(g, axis=-1)
    causal = jnp.tril(jnp.ones((CHUNK, CHUNK), dtype=bool))
    strict = jnp.tril(jnp.ones((CHUNK, CHUNK), dtype=bool), -1)
    decay = jnp.exp(jnp.where(causal, gc[..., :, None] - gc[..., None, :], -jnp.inf))
    kk = jnp.einsum('bhnid,bhnjd->bhnij', k, k)
    m = jnp.where(strict, beta[..., :, None] * kk * decay, 0.0)
    eye = jnp.eye(CHUNK, dtype=m.dtype)
    rhs = jnp.concatenate([beta[..., None] * v,
                           beta[..., None] * k * jnp.exp(gc)[..., None]], axis=-1)
    sol = lax.linalg.triangular_solve(eye + m, rhs, left_side=True, lower=True, unit_diagonal=True)
    u, w = sol[..., :dv], sol[..., dv:]
    qk = jnp.where(causal, jnp.einsum('bhnid,bhnjd->bhnij', q, k) * decay, 0.0)
    q_dec = q * jnp.exp(gc)[..., None]
    k_dec = k * jnp.exp(gc[..., -1:] - gc)[..., None]
    g_end = jnp.exp(gc[..., -1])

    def step(state, inp):
        qk_c, qd_c, kd_c, u_c, w_c, ge_c = inp
        v_new = u_c - jnp.einsum('bhck,bhkv->bhcv', w_c, state)
        o = (jnp.einsum('bhck,bhkv->bhcv', qd_c, state)
             + jnp.einsum('bhij,bhjv->bhiv', qk_c, v_new))
        state = state * ge_c[..., None, None] + jnp.einsum('bhck,bhcv->bhkv', kd_c, v_new)
        return state, o

    xs = tuple(jnp.moveaxis(t, 2, 0) for t in (qk, q_dec, k_dec, u, w, g_end))
    state0 = jnp.zeros((b, h, dk, dv), q.dtype)
    _, o = lax.scan(step, state0, xs)
    return jnp.transpose(o, (1, 0, 3, 2, 4)).reshape(b, s, h, dv)


def _stick_breaking(q, k, v):
    _, _, s, d = q.shape
    scale = d ** -0.5
    outs = []
    for blk in range(s // SB_BLOCK):
        q0 = blk * SB_BLOCK
        end = q0 + SB_BLOCK
        z = jnp.einsum('bhqd,bhkd->bhqk', q[:, :, q0:end], k[:, :, :end]) * scale
        qi = q0 + jnp.arange(SB_BLOCK)[:, None]
        kj = jnp.arange(end)[None, :]
        mask = kj < qi
        log_keep = jnp.where(mask, jax.nn.log_sigmoid(-z), 0.0)
        rest = lax.cumsum(log_keep, axis=3, reverse=True) - log_keep
        a = jnp.where(mask, jnp.exp(jax.nn.log_sigmoid(z) + rest), 0.0)
        outs.append(jnp.einsum('bhqk,bhkd->bhqd', a, v[:, :, :end]))
    return jnp.concatenate(outs, axis=2)


def _layer(x, norm_gain, w_in, b_gate, conv_w, a_log, dt_bias, dn_out_gain,
           sb_q_gain, sb_k_gain, w_up_a, w_up_b, w_out):
    f32 = jnp.float32
    b, s, _ = x.shape
    xn = _rms(x, norm_gain).astype(x.dtype)
    proj = jnp.einsum('bsd,dp->bsp', xn, w_in)

    nqk = DN_HEADS * DN_DK
    qkv_a = jax.nn.silu(_causal_dwconv(proj[..., :A_QKV1], conv_w)).astype(f32)
    q_a = _l2n(qkv_a[..., :nqk].reshape(b, s, DN_HEADS, DN_DK))
    k_a = _l2n(qkv_a[..., nqk:2 * nqk].reshape(b, s, DN_HEADS, DN_DK))
    v_a = qkv_a[..., 2 * nqk:].reshape(b, s, DN_HEADS, DN_DV)
    z_a = proj[..., A_QKV1:A_Z1].astype(f32)
    beta = jax.nn.sigmoid(proj[..., A_Z1:A_B1].astype(f32))
    g = -jnp.exp(a_log.astype(f32)) * jax.nn.softplus(proj[..., A_B1:A_A1].astype(f32)
                                                       + dt_bias.astype(f32))
    o_a = _gated_delta_rule(q_a, k_a, v_a, g, beta)
    o_a = _rms(o_a, dn_out_gain).reshape(b, s, DN_WIDTH) * jax.nn.silu(z_a)
    y_a = jnp.einsum('bsc,cd->bsd', o_a.astype(x.dtype), w_up_a)

    qkv_b = proj[..., A_A1:B_QKV1].astype(f32).reshape(b, s, 3, SB_HEADS, SB_DH)
    q_b = jnp.swapaxes(_rms(qkv_b[:, :, 0], sb_q_gain), 1, 2)
    k_b = jnp.swapaxes(_rms(qkv_b[:, :, 1], sb_k_gain), 1, 2)
    v_b = jnp.swapaxes(qkv_b[:, :, 2], 1, 2)
    z_b = proj[..., B_QKV1:B_Z1].astype(f32)
    o_b = jnp.swapaxes(_stick_breaking(q_b, k_b, v_b), 1, 2).reshape(b, s, SB_WIDTH)
    o_b = o_b * jax.nn.silu(z_b)
    y_b = jnp.einsum('bsc,cd->bsd', o_b.astype(x.dtype), w_up_b)

    gates = jax.nn.sigmoid(proj[..., B_Z1:G1].astype(f32) + b_gate.astype(f32))
    merged = gates[..., :D_MODEL] * y_a.astype(f32) + gates[..., D_MODEL:] * y_b.astype(f32)
    return x + jnp.einsum('bsd,de->bse', merged.astype(x.dtype), w_out)


def setup_inputs(seed: int = 0) -> dict:
    key = jax.random.key(seed)
    ks = jax.random.split(key, 14)
    f32 = jnp.float32
    nl = DEPTH
    nrm = jax.random.normal
    x = nrm(ks[0], (BATCH, SEQ, D_MODEL), f32)
    norm_gain = 1.0 + 0.02 * nrm(ks[1], (nl, D_MODEL), f32)
    w_in = nrm(ks[2], (nl, D_MODEL, PROJ_WIDTH), f32) * D_MODEL ** -0.5
    b_gate = 0.02 * nrm(ks[3], (nl, N_BRANCH * D_MODEL), f32)
    conv_w = nrm(ks[4], (nl, DN_CONV, A_QKV1), f32) * DN_CONV ** -0.5
    a_log = jnp.log(jax.random.uniform(ks[5], (nl, DN_HEADS), f32, minval=1.0, maxval=16.0))
    dt = jnp.exp(jax.random.uniform(ks[6], (nl, DN_HEADS), f32,
                                    minval=math.log(1e-3), maxval=math.log(1e-1)))
    dt_bias = dt + jnp.log(-jnp.expm1(-dt))
    dn_out_gain = 1.0 + 0.02 * nrm(ks[7], (nl, DN_DV), f32)
    sb_q_gain = 1.0 + 0.02 * nrm(ks[8], (nl, SB_DH), f32)
    sb_k_gain = 1.0 + 0.02 * nrm(ks[9], (nl, SB_DH), f32)
    w_up_a = nrm(ks[10], (nl, DN_WIDTH, D_MODEL), f32) * DN_WIDTH ** -0.5
    w_up_b = nrm(ks[11], (nl, SB_WIDTH, D_MODEL), f32) * SB_WIDTH ** -0.5
    w_out = nrm(ks[12], (nl, D_MODEL, D_MODEL), f32) * D_MODEL ** -0.5
    return {'x': x, 'norm_gain': norm_gain, 'w_in': w_in, 'b_gate': b_gate, 'conv_w': conv_w,
            'a_log': a_log, 'dt_bias': dt_bias, 'dn_out_gain': dn_out_gain,
            'sb_q_gain': sb_q_gain, 'sb_k_gain': sb_k_gain, 'w_up_a': w_up_a,
            'w_up_b': w_up_b, 'w_out': w_out}


def reference(x, norm_gain, w_in, b_gate, conv_w, a_log, dt_bias, dn_out_gain,
              sb_q_gain, sb_k_gain, w_up_a, w_up_b, w_out):
    h = x
    for layer in range(DEPTH):
        h = _layer(h, norm_gain[layer], w_in[layer], b_gate[layer], conv_w[layer],
                   a_log[layer], dt_bias[layer], dn_out_gain[layer], sb_q_gain[layer],
                   sb_k_gain[layer], w_up_a[layer], w_up_b[layer], w_out[layer])
    return h
```

```python
import functools

import jax
import jax.numpy as jnp
from jax import lax
from jax.experimental import pallas as pl
from jax.experimental.pallas import tpu as pltpu

F32 = jnp.float32
BF16 = jnp.bfloat16

D_MODEL = 1024
CHUNK = 64
SB_BLOCK = 128
EPS = 1e-6
DN_HEADS = 4
DN_DK = 128
DN_DV = 128
DN_CONV = 4
SB_HEADS = 8
SB_DH = 64
DN_WIDTH = DN_HEADS * DN_DV
SB_WIDTH = SB_HEADS * SB_DH
A_QKV = 2 * DN_HEADS * DN_DK + DN_WIDTH
BA_COL0 = A_QKV + DN_WIDTH
BA_COLS = 2 * DN_HEADS
MAIN_WIDTH = A_QKV + DN_WIDTH + 3 * SB_WIDTH + SB_WIDTH + 2 * D_MODEL

LANES = 128
PAIR = 2 * CHUNK

IN_TM, IN_TN = 1024, 512
DN_TS = 256
OUT_TM = 512

VMEM_LIMIT = 48 * 1024 * 1024


def _bdot(a, b):
    return jnp.dot(a.astype(BF16), b.astype(BF16), preferred_element_type=F32)


def _bdot_nt(a, b):
    return lax.dot_general(a.astype(BF16), b.astype(BF16), (((1,), (1,)), ((), ())),
                           preferred_element_type=F32)


def _bdot_tn(a, b):
    return lax.dot_general(a.astype(BF16), b.astype(BF16), (((0,), (0,)), ((), ())),
                           preferred_element_type=F32)


def _split3(x):
    hi = x.astype(BF16)
    r1 = x - hi.astype(F32)
    mid = r1.astype(BF16)
    lo = (r1 - mid.astype(F32)).astype(BF16)
    return hi, mid, lo


def _split2(x):
    hi = x.astype(BF16)
    lo = (x - hi.astype(F32)).astype(BF16)
    return hi, lo


def _silu(x):
    return x * jax.nn.sigmoid(x)


def _inproj_kernel(x_ref, gain_ref, w_ref, wba_hi_ref, wba_lo_ref, proj_ref, ba_ref, xn_ref):
    @pl.when(pl.program_id(1) == 0)
    def _():
        x = x_ref[...]
        xn = x * lax.rsqrt(jnp.mean(x * x, axis=-1, keepdims=True) + EPS) * gain_ref[...]
        hi, lo = _split2(xn)
        xn_ref[...] = hi
        ba_ref[...] = (jnp.dot(hi, wba_hi_ref[...], preferred_element_type=F32)
                       + jnp.dot(lo, wba_hi_ref[...], preferred_element_type=F32)
                       + jnp.dot(hi, wba_lo_ref[...], preferred_element_type=F32))

    proj_ref[...] = jnp.dot(xn_ref[...], w_ref[...], preferred_element_type=F32).astype(proj_ref.dtype)


def _inproj(x2, gain, w_main, wba_hi, wba_lo):
    t = x2.shape[0]
    return pl.pallas_call(
        _inproj_kernel,
        grid=(t // IN_TM, MAIN_WIDTH // IN_TN),
        in_specs=[
            pl.BlockSpec((IN_TM, D_MODEL), lambda i, j: (i, 0)),
            pl.BlockSpec((1, D_MODEL), lambda i, j: (0, 0)),
            pl.BlockSpec((D_MODEL, IN_TN), lambda i, j: (0, j)),
            pl.BlockSpec((D_MODEL, LANES), lambda i, j: (0, 0)),
            pl.BlockSpec((D_MODEL, LANES), lambda i, j: (0, 0)),
        ],
        out_specs=[
            pl.BlockSpec((IN_TM, IN_TN), lambda i, j: (i, j)),
            pl.BlockSpec((IN_TM, LANES), lambda i, j: (i, 0)),
        ],
        out_shape=[
            jax.ShapeDtypeStruct((t, MAIN_WIDTH), BF16),
            jax.ShapeDtypeStruct((t, LANES), F32),
        ],
        scratch_shapes=[pltpu.VMEM((IN_TM, D_MODEL), BF16)],
        compiler_params=pltpu.CompilerParams(
            dimension_semantics=("arbitrary", "arbitrary"), vmem_limit_bytes=VMEM_LIMIT),
        name="inproj",
    )(x2, gain, w_main, wba_hi, wba_lo)


def _tri_inverse(m, row, col):
    blk8 = (row >> 3) == (col >> 3)
    pd = jnp.where(blk8, -m, 0.0)
    x = jnp.where(row == col, 1.0, 0.0) + pd
    p2 = _bdot(pd, pd)
    x = x + _bdot(x, p2)
    p4 = _bdot(p2, p2)
    x = x + _bdot(x, p4)
    for s in (8, 16, 32):
        sh = s.bit_length()
        off = ((row >> sh) == (col >> sh)) & ((row & (2 * s - 1)) >= s) & ((col & (2 * s - 1)) < s)
        x = x - _bdot(_bdot(x, jnp.where(off, m, 0.0)), x)
    return x


def _deltanet_kernel(qkv_ref, z_ref, ba_ref, convw_ref, par_ref, gain_ref, o_ref,
                     buf_ref, act_ref, state_ref):
    ts = qkv_ref.shape[0]
    nqk = DN_HEADS * DN_DK

    @pl.when(pl.program_id(1) == 0)
    def _():
        buf_ref[0:8, :] = jnp.zeros((8, A_QKV), F32)
        state_ref[...] = jnp.zeros_like(state_ref)

    buf_ref[8:8 + ts, :] = qkv_ref[...].astype(F32)
    for s in range(A_QKV // LANES):
        cs = slice(s * LANES, (s + 1) * LANES)
        acc = None
        for i in range(DN_CONV):
            term = buf_ref[8 - (DN_CONV - 1) + i:8 - (DN_CONV - 1) + i + ts, cs] * convw_ref[i:i + 1, cs]
            acc = term if acc is None else acc + term
        y = _silu(acc)
        if s < 2 * DN_HEADS:
            y = y * lax.rsqrt(jnp.sum(y * y, axis=-1, keepdims=True) + EPS)
        if s < DN_HEADS:
            y = y * (DN_DK ** -0.5)
        act_ref[:, cs] = y
    buf_ref[0:8, :] = buf_ref[ts:ts + 8, :]

    ba = ba_ref[...]
    beta_all = jax.nn.sigmoid(ba)
    xs = ba + par_ref[1:2, :]
    softplus = jnp.maximum(xs, 0.0) + jnp.log1p(jnp.exp(-jnp.abs(xs)))
    g_all = -jnp.exp(par_ref[0:1, :]) * softplus

    row = lax.broadcasted_iota(jnp.int32, (PAIR, PAIR), 0)
    col = lax.broadcasted_iota(jnp.int32, (PAIR, PAIR), 1)
    same = (row >> 6) == (col >> 6)
    lower = same & (row >= col)
    strict = same & (row > col)
    ltri = jnp.where(lower, 1.0, 0.0).astype(BF16)
    ltri3 = jnp.concatenate([ltri, ltri, ltri], axis=1)
    gain = gain_ref[...]

    for p in range(ts // PAIR):
        rs = slice(p * PAIR, (p + 1) * PAIR)
        for h in range(DN_HEADS):
            hs = slice(h * LANES, (h + 1) * LANES)
            q = act_ref[rs, h * DN_DK:(h + 1) * DN_DK]
            k = act_ref[rs, nqk + h * DN_DK:nqk + (h + 1) * DN_DK]
            v = act_ref[rs, 2 * nqk + h * DN_DV:2 * nqk + (h + 1) * DN_DV]
            beta = beta_all[rs, h:h + 1]
            gb = jnp.broadcast_to(g_all[rs, DN_HEADS + h:DN_HEADS + h + 1], (PAIR, PAIR))
            hi, mid, lo = _split3(jnp.concatenate([jnp.where(strict, gb, 0.0), gb], axis=1))
            cum = jnp.dot(ltri3, jnp.concatenate([hi, mid, lo], axis=0), preferred_element_type=F32)
            gdiff = cum[:, :PAIR]
            gc = cum[:, PAIR:]
            decay = jnp.where(lower, jnp.exp(gdiff), 0.0)
            eg = jnp.exp(gc)
            gc_end = jnp.where(row < CHUNK, gc[CHUNK - 1:CHUNK, :], gc[PAIR - 1:PAIR, :])

            qkk = _bdot_nt(jnp.concatenate([q, k], axis=0), k)
            qk = jnp.where(lower, qkk[:PAIR] * decay, 0.0)
            m = jnp.where(strict, beta * qkk[PAIR:] * decay, 0.0)
            tinv = _tri_inverse(m, row, col)
            sol = _bdot(tinv, jnp.concatenate([beta * v, beta * k * eg], axis=1))
            u = sol[:, :DN_DV]
            w = sol[:, DN_DV:]
            qd = q * eg
            kd = k * jnp.exp(gc_end - gc)

            st = state_ref[h]
            for c in range(PAIR // CHUNK):
                cr = slice(c * CHUNK, (c + 1) * CHUNK)
                ws = _bdot(jnp.concatenate([w[cr], qd[cr]], axis=0), st)
                v_new = u[cr] - ws[:CHUNK]
                zeros = jnp.zeros_like(v_new)
                v_pad = jnp.concatenate([v_new, zeros] if c == 0 else [zeros, v_new], axis=0)
                o = ws[CHUNK:] + _bdot(qk[cr], v_pad)
                g_end = jnp.exp(gc[(c + 1) * CHUNK - 1:(c + 1) * CHUNK, :])
                st = st * g_end + _bdot_tn(kd[cr], v_new)
                o = o * lax.rsqrt(jnp.mean(o * o, axis=-1, keepdims=True) + EPS) * gain
                zz = z_ref[p * PAIR + c * CHUNK:p * PAIR + (c + 1) * CHUNK, hs].astype(F32)
                o_ref[p * PAIR + c * CHUNK:p * PAIR + (c + 1) * CHUNK, hs] = (o * _silu(zz)).astype(o_ref.dtype)
            state_ref[h] = st


def _deltanet(proj, ba, conv_w, par, gain, batch, seq):
    nsb = seq // DN_TS
    return pl.pallas_call(
        _deltanet_kernel,
        grid=(batch, nsb),
        in_specs=[
            pl.BlockSpec((DN_TS, A_QKV), lambda b, s: (b * nsb + s, 0)),
            pl.BlockSpec((DN_TS, DN_WIDTH), lambda b, s: (b * nsb + s, A_QKV // DN_WIDTH)),
            pl.BlockSpec((DN_TS, LANES), lambda b, s: (b * nsb + s, 0)),
            pl.BlockSpec((DN_CONV, A_QKV), lambda b, s: (0, 0)),
            pl.BlockSpec((8, LANES), lambda b, s: (0, 0)),
            pl.BlockSpec((1, DN_DV), lambda b, s: (0, 0)),
        ],
        out_specs=pl.BlockSpec((DN_TS, DN_WIDTH), lambda b, s: (b * nsb + s, 0)),
        out_shape=jax.ShapeDtypeStruct((batch * seq, DN_WIDTH), BF16),
        scratch_shapes=[
            pltpu.VMEM((DN_TS + 8, A_QKV), F32),
            pltpu.VMEM((DN_TS, A_QKV), F32),
            pltpu.VMEM((DN_HEADS, DN_DK, DN_DV), F32),
        ],
        compiler_params=pltpu.CompilerParams(
            dimension_semantics=("arbitrary", "arbitrary"), vmem_limit_bytes=VMEM_LIMIT),
        name="deltanet",
    )(proj, proj, ba, conv_w, par, gain)


def _rms_halves(x, half_ones2):
    hi, lo = _split2(x * x)
    ss = jnp.dot(jnp.concatenate([hi, lo], axis=1), half_ones2, preferred_element_type=F32)
    return x * lax.rsqrt(ss * (1.0 / SB_DH) + EPS)


def _sb_kernel(q_ref, k_ref, v_ref, qg_ref, kg_ref, cum_ref, half_ref, o_ref,
               kn_ref, v2_ref, r_ref, acc_ref):
    i = pl.program_id(2)
    nblk = k_ref.shape[0] // SB_BLOCK
    lane = lax.broadcasted_iota(jnp.int32, (SB_BLOCK, LANES), 1)
    first = lane < SB_DH

    @pl.when(i == 0)
    def _():
        def prep(j, carry):
            rows = pl.ds(pl.multiple_of(j * SB_BLOCK, SB_BLOCK), SB_BLOCK)
            kn = _rms_halves(k_ref[rows, :].astype(F32), half_ref[...]) * kg_ref[...]
            kn_ref[rows, :] = kn.astype(BF16)
            vb = v_ref[rows, :]
            zero = jnp.zeros_like(vb)
            v2_ref[j, 0:SB_BLOCK, :] = jnp.where(first, vb, zero)
            v2_ref[j, SB_BLOCK:2 * SB_BLOCK, :] = jnp.where(first, zero, vb)
            return carry
        lax.fori_loop(0, nblk, prep, 0)

    qn = _rms_halves(q_ref[...].astype(F32), half_ref[...]) * (qg_ref[...] * (SB_DH ** -0.5))
    q2 = jnp.concatenate([jnp.where(first, qn, 0.0), jnp.where(first, 0.0, qn)], axis=0).astype(BF16)

    def tile(kblk, v2blk, r, acc, mask):
        z = lax.dot_general(q2, kblk, (((1,), (1,)), ((), ())), preferred_element_type=F32)
        ls = jnp.minimum(z, 0.0) - jnp.log1p(jnp.exp(-jnp.abs(z)))
        lk = ls - z
        if mask is not None:
            lk = jnp.where(mask, lk, 0.0)
        hi, lo = _split2(lk)
        cs = jnp.dot(jnp.concatenate([hi, lo], axis=1), cum_ref[...], preferred_element_type=F32)
        a = jnp.exp(ls + cs[:, :SB_BLOCK] + r)
        if mask is not None:
            a = jnp.where(mask, a, 0.0)
        ab = a.astype(BF16)
        a2 = jnp.concatenate([ab[:SB_BLOCK], ab[SB_BLOCK:]], axis=1)
        return r + cs[:, SB_BLOCK:], acc + jnp.dot(a2, v2blk, preferred_element_type=F32)

    qrow = lax.broadcasted_iota(jnp.int32, (2 * SB_BLOCK, SB_BLOCK), 0) & (SB_BLOCK - 1)
    kcol = lax.broadcasted_iota(jnp.int32, (2 * SB_BLOCK, SB_BLOCK), 1)
    rows_i = pl.ds(pl.multiple_of(i * SB_BLOCK, SB_BLOCK), SB_BLOCK)
    r0, acc0 = tile(kn_ref[rows_i, :], v2_ref[i],
                    jnp.zeros((2 * SB_BLOCK, SB_BLOCK), F32), jnp.zeros((SB_BLOCK, LANES), F32),
                    kcol < qrow)
    r_ref[...] = r0
    acc_ref[...] = acc0

    def body(t, carry):
        j = i - 1 - t
        rows = pl.ds(pl.multiple_of(j * SB_BLOCK, SB_BLOCK), SB_BLOCK)
        r1, acc1 = tile(kn_ref[rows, :], v2_ref[j], r_ref[...], acc_ref[...], None)
        r_ref[...] = r1
        acc_ref[...] = acc1
        return carry
    lax.fori_loop(0, i, body, 0)

    o_ref[...] = acc_ref[...].astype(o_ref.dtype)


def _stick_breaking(proj, qg2, kg2, cum_mat, half_ones2, batch, seq):
    nq = seq // SB_BLOCK
    npair = SB_HEADS // 2
    qcol = (A_QKV + DN_WIDTH) // LANES
    kcol = qcol + SB_WIDTH // LANES
    vcol = kcol + SB_WIDTH // LANES
    return pl.pallas_call(
        _sb_kernel,
        grid=(batch, npair, nq),
        in_specs=[
            pl.BlockSpec((SB_BLOCK, LANES), lambda b, p, i: (b * nq + i, qcol + p)),
            pl.BlockSpec((seq, LANES), lambda b, p, i: (b, kcol + p)),
            pl.BlockSpec((seq, LANES), lambda b, p, i: (b, vcol + p)),
            pl.BlockSpec((1, LANES), lambda b, p, i: (0, 0)),
            pl.BlockSpec((1, LANES), lambda b, p, i: (0, 0)),
            pl.BlockSpec((2 * SB_BLOCK, 2 * SB_BLOCK), lambda b, p, i: (0, 0)),
            pl.BlockSpec((2 * LANES, LANES), lambda b, p, i: (0, 0)),
        ],
        out_specs=pl.BlockSpec((SB_BLOCK, LANES), lambda b, p, i: (b * nq + i, p)),
        out_shape=jax.ShapeDtypeStruct((batch * seq, SB_WIDTH), BF16),
        scratch_shapes=[
            pltpu.VMEM((seq, LANES), BF16),
            pltpu.VMEM((nq, 2 * SB_BLOCK, LANES), BF16),
            pltpu.VMEM((2 * SB_BLOCK, SB_BLOCK), F32),
            pltpu.VMEM((SB_BLOCK, LANES), F32),
        ],
        compiler_params=pltpu.CompilerParams(
            dimension_semantics=("arbitrary", "arbitrary", "arbitrary"), vmem_limit_bytes=VMEM_LIMIT),
        name="stickbreak",
    )(proj, proj, proj, qg2, kg2, cum_mat, half_ones2)


def _merge_kernel(x_ref, oa_ref, ob_ref, zb_ref, g_ref, bg_ref, wa_ref, wb_ref, wo_ref, out_ref):
    zb = zb_ref[...].astype(F32)
    ob = (ob_ref[...].astype(F32) * _silu(zb)).astype(BF16)
    ya = jnp.dot(oa_ref[...], wa_ref[...], preferred_element_type=F32)
    yb = jnp.dot(ob, wb_ref[...], preferred_element_type=F32)
    gates = jax.nn.sigmoid(g_ref[...].astype(F32) + bg_ref[...])
    merged = gates[:, :D_MODEL] * ya + gates[:, D_MODEL:] * yb
    out_ref[...] = x_ref[...] + jnp.dot(merged.astype(BF16), wo_ref[...], preferred_element_type=F32)


def _merge(x2, o_a, o_b, proj, b_gate, w_up_a, w_up_b, w_out):
    t = x2.shape[0]
    zb_col = (A_QKV + DN_WIDTH + 3 * SB_WIDTH) // SB_WIDTH
    g_col = (A_QKV + DN_WIDTH + 4 * SB_WIDTH) // (2 * D_MODEL)
    const = lambda i: (0, 0)
    return pl.pallas_call(
        _merge_kernel,
        grid=(t // OUT_TM,),
        in_specs=[
            pl.BlockSpec((OUT_TM, D_MODEL), lambda i: (i, 0)),
            pl.BlockSpec((OUT_TM, DN_WIDTH), lambda i: (i, 0)),
            pl.BlockSpec((OUT_TM, SB_WIDTH), lambda i: (i, 0)),
            pl.BlockSpec((OUT_TM, SB_WIDTH), lambda i: (i, zb_col)),
            pl.BlockSpec((OUT_TM, 2 * D_MODEL), lambda i: (i, g_col)),
            pl.BlockSpec((1, 2 * D_MODEL), const),
            pl.BlockSpec((DN_WIDTH, D_MODEL), const),
            pl.BlockSpec((SB_WIDTH, D_MODEL), const),
            pl.BlockSpec((D_MODEL, D_MODEL), const),
        ],
        out_specs=pl.BlockSpec((OUT_TM, D_MODEL), lambda i: (i, 0)),
        out_shape=jax.ShapeDtypeStruct((t, D_MODEL), F32),
        compiler_params=pltpu.CompilerParams(
            dimension_semantics=("arbitrary",), vmem_limit_bytes=VMEM_LIMIT),
        name="merge",
    )(x2, o_a, o_b, proj, proj, b_gate, w_up_a, w_up_b, w_out)


def _layer(x, norm_gain, w_in, b_gate, conv_w, a_log, dt_bias, dn_out_gain,
           sb_q_gain, sb_k_gain, w_up_a, w_up_b, w_out):
    batch, seq, _ = x.shape
    x2 = x.reshape(batch * seq, D_MODEL)

    w_main = jnp.concatenate([w_in[:, :BA_COL0], w_in[:, BA_COL0 + BA_COLS:]], axis=1).astype(BF16)
    w_ba = jnp.pad(w_in[:, BA_COL0:BA_COL0 + BA_COLS], ((0, 0), (0, LANES - BA_COLS)))
    wba_hi = w_ba.astype(BF16)
    wba_lo = (w_ba - wba_hi.astype(F32)).astype(BF16)
    par = jnp.zeros((8, LANES), F32)
    par = par.at[0, DN_HEADS:2 * DN_HEADS].set(a_log).at[1, DN_HEADS:2 * DN_HEADS].set(dt_bias)
    qg2 = jnp.concatenate([sb_q_gain, sb_q_gain])[None, :]
    kg2 = jnp.concatenate([sb_k_gain, sb_k_gain])[None, :]
    kk = jnp.arange(SB_BLOCK)
    cum1 = jnp.concatenate([(kk[:, None] > kk[None, :]).astype(BF16), jnp.ones((SB_BLOCK, SB_BLOCK), BF16)], axis=1)
    cum_mat = jnp.concatenate([cum1, cum1], axis=0)
    ln = jnp.arange(LANES)
    half1 = ((ln[:, None] // SB_DH) == (ln[None, :] // SB_DH)).astype(BF16)
    half_ones2 = jnp.concatenate([half1, half1], axis=0)

    proj, ba = _inproj(x2, norm_gain[None, :], w_main, wba_hi, wba_lo)
    o_a = _deltanet(proj, ba, conv_w, par, dn_out_gain[None, :], batch, seq)
    o_b = _stick_breaking(proj, qg2, kg2, cum_mat, half_ones2, batch, seq)
    out = _merge(x2, o_a, o_b, proj, b_gate[None, :], w_up_a.astype(BF16), w_up_b.astype(BF16),
                 w_out.astype(BF16))
    return out.reshape(batch, seq, D_MODEL)


def kernel(x, norm_gain, w_in, b_gate, conv_w, a_log, dt_bias, dn_out_gain, sb_q_gain, sb_k_gain,
           w_up_a, w_up_b, w_out):
    h = x
    for layer in range(norm_gain.shape[0]):
        h = _layer(h, norm_gain[layer], w_in[layer], b_gate[layer], conv_w[layer], a_log[layer],
                   dt_bias[layer], dn_out_gain[layer], sb_q_gain[layer], sb_k_gain[layer],
                   w_up_a[layer], w_up_b[layer], w_out[layer])
    return h
```

```python
import jax
import jax.numpy as jnp
from jax import lax
from jax.experimental import pallas as pl
from jax.experimental.pallas import tpu as pltpu

F32 = jnp.float32
BF16 = jnp.bfloat16

D_MODEL = 1024
CHUNK = 64
SB_BLOCK = 128
EPS = 1e-6
DN_HEADS = 4
DN_DK = 128
DN_DV = 128
DN_CONV = 4
SB_HEADS = 8
SB_DH = 64
DN_WIDTH = DN_HEADS * DN_DV
SB_WIDTH = SB_HEADS * SB_DH
A_QKV = 2 * DN_HEADS * DN_DK + DN_WIDTH
BA_COL0 = A_QKV + DN_WIDTH
BA_COLS = 2 * DN_HEADS
MAIN_WIDTH = A_QKV + DN_WIDTH + 3 * SB_WIDTH + SB_WIDTH + 2 * D_MODEL

LANES = 128
PAIR = 2 * CHUNK

IN_TM, IN_TN = 1024, 512
DN_TS = 256
OUT_TM = 512

VMEM_LIMIT = 48 * 1024 * 1024


def _split3(x):
    hi = x.astype(BF16)
    r1 = x - hi.astype(F32)
    mid = r1.astype(BF16)
    lo = (r1 - mid.astype(F32)).astype(BF16)
    return hi, mid, lo


def _split2(x):
    hi = x.astype(BF16)
    lo = (x - hi.astype(F32)).astype(BF16)
    return hi, lo


def _silu(x):
    return x * jax.nn.sigmoid(x)


def _inproj_kernel(x_ref, gain_ref, w_ref, wba_hi_ref, wba_lo_ref, proj_ref, ba_ref, xn_ref):
    @pl.when(pl.program_id(1) == 0)
    def _():
        x = x_ref[...]
        xn = x * lax.rsqrt(jnp.mean(x * x, axis=-1, keepdims=True) + EPS) * gain_ref[...]
        hi, lo = _split2(xn)
        xn_ref[...] = hi
        ba_ref[...] = (jnp.dot(hi, wba_hi_ref[...], preferred_element_type=F32)
                       + jnp.dot(lo, wba_hi_ref[...], preferred_element_type=F32)
                       + jnp.dot(hi, wba_lo_ref[...], preferred_element_type=F32))

    proj_ref[...] = jnp.dot(xn_ref[...], w_ref[...], preferred_element_type=F32).astype(proj_ref.dtype)


def _inproj(x2, gain, w_main, wba_hi, wba_lo):
    t = x2.shape[0]
    return pl.pallas_call(
        _inproj_kernel,
        grid=(t // IN_TM, MAIN_WIDTH // IN_TN),
        in_specs=[
            pl.BlockSpec((IN_TM, D_MODEL), lambda i, j: (i, 0)),
            pl.BlockSpec((1, D_MODEL), lambda i, j: (0, 0)),
            pl.BlockSpec((D_MODEL, IN_TN), lambda i, j: (0, j)),
            pl.BlockSpec((D_MODEL, LANES), lambda i, j: (0, 0)),
            pl.BlockSpec((D_MODEL, LANES), lambda i, j: (0, 0)),
        ],
        out_specs=[
            pl.BlockSpec((IN_TM, IN_TN), lambda i, j: (i, j)),
            pl.BlockSpec((IN_TM, LANES), lambda i, j: (i, 0)),
        ],
        out_shape=[
            jax.ShapeDtypeStruct((t, MAIN_WIDTH), BF16),
            jax.ShapeDtypeStruct((t, LANES), F32),
        ],
        scratch_shapes=[pltpu.VMEM((IN_TM, D_MODEL), BF16)],
        compiler_params=pltpu.CompilerParams(
            dimension_semantics=("arbitrary", "arbitrary"), vmem_limit_bytes=VMEM_LIMIT),
        name="inproj",
    )(x2, gain, w_main, wba_hi, wba_lo)


def _tri_inverse(ms, row, col):
    blk8 = (row >> 3) == (col >> 3)
    eye = jnp.where(row == col, 1.0, 0.0)
    pds = [jnp.where(blk8, -m, 0.0) for m in ms]
    pdb = [pd.astype(BF16) for pd in pds]
    xs = [eye + pd for pd in pds]
    p2b = [jnp.dot(b, b, preferred_element_type=F32).astype(BF16) for b in pdb]
    xs = [x + jnp.dot(x.astype(BF16), p2, preferred_element_type=F32) for x, p2 in zip(xs, p2b)]
    p4b = [jnp.dot(b, b, preferred_element_type=F32).astype(BF16) for b in p2b]
    xs = [x + jnp.dot(x.astype(BF16), p4, preferred_element_type=F32) for x, p4 in zip(xs, p4b)]
    for s in (8, 16, 32):
        sh = s.bit_length()
        off = ((row >> sh) == (col >> sh)) & ((row & (2 * s - 1)) >= s) & ((col & (2 * s - 1)) < s)
        xb = [x.astype(BF16) for x in xs]
        ts_ = [jnp.dot(b, jnp.where(off, m, 0.0).astype(BF16), preferred_element_type=F32).astype(BF16)
               for b, m in zip(xb, ms)]
        xs = [x - jnp.dot(t, b, preferred_element_type=F32) for x, t, b in zip(xs, ts_, xb)]
    return xs


def _deltanet_kernel(qkv_ref, z_ref, ba_ref, convw_ref, par_ref, gain_ref, ltri_ref, o_ref,
                     buf_ref, act_ref, state_ref, u_ref, wq_ref, qk_ref, kd_ref, ge_ref):
    ts = qkv_ref.shape[0]
    nqk = DN_HEADS * DN_DK
    heads = range(DN_HEADS)

    @pl.when(pl.program_id(1) == 0)
    def _():
        buf_ref[0:8, :] = jnp.zeros((8, A_QKV), F32)
        state_ref[...] = jnp.zeros_like(state_ref)

    buf_ref[8:8 + ts, :] = qkv_ref[...].astype(F32)
    for s in range(A_QKV // LANES):
        cs = slice(s * LANES, (s + 1) * LANES)
        acc = None
        for i in range(DN_CONV):
            term = buf_ref[8 - (DN_CONV - 1) + i:8 - (DN_CONV - 1) + i + ts, cs] * convw_ref[i:i + 1, cs]
            acc = term if acc is None else acc + term
        y = _silu(acc)
        if s < 2 * DN_HEADS:
            y = y * lax.rsqrt(jnp.sum(y * y, axis=-1, keepdims=True) + EPS)
        if s < DN_HEADS:
            y = y * (DN_DK ** -0.5)
        act_ref[:, cs] = y
    buf_ref[0:8, :] = buf_ref[ts:ts + 8, :]

    ba = ba_ref[...]
    beta_all = jax.nn.sigmoid(ba)
    xs = ba + par_ref[1:2, :]
    softplus = jnp.maximum(xs, 0.0) + jnp.log(1.0 + jnp.exp(-jnp.abs(xs)))
    g_all = -jnp.exp(par_ref[0:1, :]) * softplus
    gc_all = jnp.dot(ltri_ref[...], jnp.concatenate(_split3(g_all), axis=0), preferred_element_type=F32)

    row = lax.broadcasted_iota(jnp.int32, (PAIR, PAIR), 0)
    col = lax.broadcasted_iota(jnp.int32, (PAIR, PAIR), 1)
    same = (row >> 6) == (col >> 6)
    lower = same & (row >= col)
    strict = same & (row > col)
    rcol = lax.broadcasted_iota(jnp.int32, (PAIR, 1), 0)

    for p in range(ts // PAIR):
        rs = slice(p * PAIR, (p + 1) * PAIR)
        gc_t = gc_all[rs, :].T
        q = [act_ref[rs, h * DN_DK:(h + 1) * DN_DK] for h in heads]
        k = [act_ref[rs, nqk + h * DN_DK:nqk + (h + 1) * DN_DK] for h in heads]
        v = [act_ref[rs, 2 * nqk + h * DN_DV:2 * nqk + (h + 1) * DN_DV] for h in heads]
        beta = [beta_all[rs, h:h + 1] for h in heads]
        gcc = [gc_all[rs, DN_HEADS + h:DN_HEADS + h + 1] for h in heads]
        decay = [jnp.where(lower, jnp.exp(gcc[h] - gc_t[DN_HEADS + h:DN_HEADS + h + 1, :]), 0.0) for h in heads]
        eg = [jnp.exp(g) for g in gcc]
        gend = [jnp.where(rcol < CHUNK, g[CHUNK - 1:CHUNK, :], g[PAIR - 1:PAIR, :]) for g in gcc]
        kb = [x.astype(BF16) for x in k]
        qkk = [lax.dot_general(jnp.concatenate([q[h].astype(BF16), kb[h]], axis=0), kb[h],
                               (((1,), (1,)), ((), ())), preferred_element_type=F32) for h in heads]
        ms = [jnp.where(strict, beta[h] * qkk[h][PAIR:] * decay[h], 0.0) for h in heads]
        tinv = _tri_inverse(ms, row, col)
        sol = [jnp.dot(tinv[h].astype(BF16),
                       jnp.concatenate([(beta[h] * v[h]).astype(BF16), (beta[h] * eg[h] * k[h]).astype(BF16)], axis=1),
                       preferred_element_type=F32) for h in heads]
        for h in heads:
            u_ref[h, rs, :] = sol[h][:, :DN_DV]
            qd = (q[h] * eg[h]).astype(BF16)
            wb = sol[h][:, DN_DV:].astype(BF16)
            qk = jnp.where(lower, qkk[h][:PAIR] * decay[h], 0.0).astype(BF16)
            kd_ref[h, rs, :] = (k[h] * jnp.exp(gend[h] - gcc[h])).astype(BF16)
            for c in range(PAIR // CHUNK):
                cr = slice(c * CHUNK, (c + 1) * CHUNK)
                cc = p * (PAIR // CHUNK) + c
                wq_ref[h, cc, 0:CHUNK, :] = wb[cr]
                wq_ref[h, cc, CHUNK:PAIR, :] = qd[cr]
                qk_ref[h, cc] = qk[cr, c * CHUNK:(c + 1) * CHUNK]
                ge_ref[h, cc] = jnp.broadcast_to(
                    jnp.exp(gcc[h][(c + 1) * CHUNK - 1:(c + 1) * CHUNK, :]), (8, LANES))

    gain = gain_ref[...]
    st = [state_ref[h] for h in heads]
    for cc in range(ts // CHUNK):
        cr = slice(cc * CHUNK, (cc + 1) * CHUNK)
        ws = [jnp.dot(wq_ref[h, cc], st[h].astype(BF16), preferred_element_type=F32) for h in heads]
        vn = [(u_ref[h, cr, :] - ws[h][:CHUNK]).astype(BF16) for h in heads]
        o = [ws[h][CHUNK:] + jnp.dot(qk_ref[h, cc], vn[h], preferred_element_type=F32) for h in heads]
        st = [st[h] * ge_ref[h, cc, 0:1, :]
              + lax.dot_general(kd_ref[h, cr, :], vn[h], (((0,), (0,)), ((), ())), preferred_element_type=F32)
              for h in heads]
        for h in heads:
            hs = slice(h * DN_DV, (h + 1) * DN_DV)
            on = o[h] * lax.rsqrt(jnp.mean(o[h] * o[h], axis=-1, keepdims=True) + EPS) * gain
            zz = z_ref[cr, hs].astype(F32)
            o_ref[cr, hs] = (on * _silu(zz)).astype(o_ref.dtype)
    for h in heads:
        state_ref[h] = st[h]


def _deltanet(proj, ba, conv_w, par, gain, ltri3, batch, seq):
    nsb = seq // DN_TS
    nchunk = DN_TS // CHUNK
    return pl.pallas_call(
        _deltanet_kernel,
        grid=(batch, nsb),
        in_specs=[
            pl.BlockSpec((DN_TS, A_QKV), lambda b, s: (b * nsb + s, 0)),
            pl.BlockSpec((DN_TS, DN_WIDTH), lambda b, s: (b * nsb + s, A_QKV // DN_WIDTH)),
            pl.BlockSpec((DN_TS, LANES), lambda b, s: (b * nsb + s, 0)),
            pl.BlockSpec((DN_CONV, A_QKV), lambda b, s: (0, 0)),
            pl.BlockSpec((8, LANES), lambda b, s: (0, 0)),
            pl.BlockSpec((1, DN_DV), lambda b, s: (0, 0)),
            pl.BlockSpec((DN_TS, 3 * DN_TS), lambda b, s: (0, 0)),
        ],
        out_specs=pl.BlockSpec((DN_TS, DN_WIDTH), lambda b, s: (b * nsb + s, 0)),
        out_shape=jax.ShapeDtypeStruct((batch * seq, DN_WIDTH), BF16),
        scratch_shapes=[
            pltpu.VMEM((DN_TS + 8, A_QKV), F32),
            pltpu.VMEM((DN_TS, A_QKV), F32),
            pltpu.VMEM((DN_HEADS, DN_DK, DN_DV), F32),
            pltpu.VMEM((DN_HEADS, DN_TS, DN_DV), F32),
            pltpu.VMEM((DN_HEADS, nchunk, PAIR, DN_DK), BF16),
            pltpu.VMEM((DN_HEADS, nchunk, CHUNK, CHUNK), BF16),
            pltpu.VMEM((DN_HEADS, DN_TS, DN_DK), BF16),
            pltpu.VMEM((DN_HEADS, nchunk, 8, LANES), F32),
        ],
        compiler_params=pltpu.CompilerParams(
            dimension_semantics=("arbitrary", "arbitrary"), vmem_limit_bytes=VMEM_LIMIT),
        name="deltanet",
    )(proj, proj, ba, conv_w, par, gain, ltri3)


def _rms_halves(x, half_ones2):
    hi, lo = _split2(x * x)
    ss = jnp.dot(jnp.concatenate([hi, lo], axis=1), half_ones2, preferred_element_type=F32)
    return x * lax.rsqrt(ss * (1.0 / SB_DH) + EPS)


def _sb_kernel(q_ref, k_ref, v_ref, qg_ref, kg_ref, cum_ref, half_ref, o_ref,
               kn_ref, v2_ref, r_ref, acc_ref):
    i = pl.program_id(1)
    nblk = k_ref.shape[0] // SB_BLOCK
    npair = q_ref.shape[1] // LANES
    pairs = range(npair)
    lane = lax.broadcasted_iota(jnp.int32, (SB_BLOCK, LANES), 1)
    first = lane < SB_DH

    def cols(p):
        return slice(p * LANES, (p + 1) * LANES)

    @pl.when(i == 0)
    def _():
        def prep(j, carry):
            rows = pl.ds(pl.multiple_of(j * SB_BLOCK, SB_BLOCK), SB_BLOCK)
            for p in pairs:
                kn = _rms_halves(k_ref[rows, cols(p)].astype(F32), half_ref[...]) * kg_ref[...]
                kn_ref[rows, cols(p)] = kn.astype(BF16)
                vb = v_ref[rows, cols(p)]
                zero = jnp.zeros_like(vb)
                v2_ref[p, j, 0:SB_BLOCK, :] = jnp.where(first, vb, zero)
                v2_ref[p, j, SB_BLOCK:2 * SB_BLOCK, :] = jnp.where(first, zero, vb)
            return carry
        lax.fori_loop(0, nblk, prep, 0)

    qscale = qg_ref[...] * (SB_DH ** -0.5)
    q2 = []
    for p in pairs:
        qn = _rms_halves(q_ref[:, cols(p)].astype(F32), half_ref[...]) * qscale
        q2.append(jnp.concatenate([jnp.where(first, qn, 0.0), jnp.where(first, 0.0, qn)], axis=0).astype(BF16))

    def tiles(rows, j, rs, accs, mask):
        zs = [lax.dot_general(q2[p], kn_ref[rows, cols(p)], (((1,), (1,)), ((), ())),
                              preferred_element_type=F32) for p in pairs]
        lss = [jnp.minimum(z, 0.0) - jnp.log(1.0 + jnp.exp(-jnp.abs(z))) for z in zs]
        lks = [ls - z for ls, z in zip(lss, zs)]
        if mask is not None:
            lks = [jnp.where(mask, lk, 0.0) for lk in lks]
        css = [jnp.dot(jnp.concatenate(_split2(lk), axis=1), cum_ref[...], preferred_element_type=F32)
               for lk in lks]
        avs = [jnp.exp(ls + cs[:, :SB_BLOCK] + r) for ls, cs, r in zip(lss, css, rs)]
        if mask is not None:
            avs = [jnp.where(mask, a, 0.0) for a in avs]
        new_rs = [r + cs[:, SB_BLOCK:] for r, cs in zip(rs, css)]
        new_accs = []
        for p in pairs:
            ab = avs[p].astype(BF16)
            a2 = jnp.concatenate([ab[:SB_BLOCK], ab[SB_BLOCK:]], axis=1)
            new_accs.append(accs[p] + jnp.dot(a2, v2_ref[p, j], preferred_element_type=F32))
        return new_rs, new_accs

    qrow = lax.broadcasted_iota(jnp.int32, (2 * SB_BLOCK, SB_BLOCK), 0) & (SB_BLOCK - 1)
    kcol = lax.broadcasted_iota(jnp.int32, (2 * SB_BLOCK, SB_BLOCK), 1)
    rows_i = pl.ds(pl.multiple_of(i * SB_BLOCK, SB_BLOCK), SB_BLOCK)
    rs, accs = tiles(rows_i, i,
                     [jnp.zeros((2 * SB_BLOCK, SB_BLOCK), F32)] * npair,
                     [jnp.zeros((SB_BLOCK, LANES), F32)] * npair, kcol < qrow)
    for p in pairs:
        r_ref[p] = rs[p]
        acc_ref[:, cols(p)] = accs[p]

    def body(t, carry):
        j = i - 1 - t
        rows = pl.ds(pl.multiple_of(j * SB_BLOCK, SB_BLOCK), SB_BLOCK)
        rs, accs = tiles(rows, j, [r_ref[p] for p in pairs], [acc_ref[:, cols(p)] for p in pairs], None)
        for p in pairs:
            r_ref[p] = rs[p]
            acc_ref[:, cols(p)] = accs[p]
        return carry
    lax.fori_loop(0, i, body, 0)

    o_ref[...] = acc_ref[...].astype(o_ref.dtype)


def _stick_breaking(proj, qg2, kg2, cum_mat, half_ones2, batch, seq):
    nq = seq // SB_BLOCK
    npair = SB_HEADS // 2
    qcol = (A_QKV + DN_WIDTH) // SB_WIDTH
    return pl.pallas_call(
        _sb_kernel,
        grid=(batch, nq),
        in_specs=[
            pl.BlockSpec((SB_BLOCK, SB_WIDTH), lambda b, i: (b * nq + i, qcol)),
            pl.BlockSpec((seq, SB_WIDTH), lambda b, i: (b, qcol + 1)),
            pl.BlockSpec((seq, SB_WIDTH), lambda b, i: (b, qcol + 2)),
            pl.BlockSpec((1, LANES), lambda b, i: (0, 0)),
            pl.BlockSpec((1, LANES), lambda b, i: (0, 0)),
            pl.BlockSpec((2 * SB_BLOCK, 2 * SB_BLOCK), lambda b, i: (0, 0)),
            pl.BlockSpec((2 * LANES, LANES), lambda b, i: (0, 0)),
        ],
        out_specs=pl.BlockSpec((SB_BLOCK, SB_WIDTH), lambda b, i: (b * nq + i, 0)),
        out_shape=jax.ShapeDtypeStruct((batch * seq, SB_WIDTH), BF16),
        scratch_shapes=[
            pltpu.VMEM((seq, SB_WIDTH), BF16),
            pltpu.VMEM((npair, nq, 2 * SB_BLOCK, LANES), BF16),
            pltpu.VMEM((npair, 2 * SB_BLOCK, SB_BLOCK), F32),
            pltpu.VMEM((SB_BLOCK, SB_WIDTH), F32),
        ],
        compiler_params=pltpu.CompilerParams(
            dimension_semantics=("arbitrary", "arbitrary"), vmem_limit_bytes=VMEM_LIMIT),
        name="stickbreak",
    )(proj, proj, proj, qg2, kg2, cum_mat, half_ones2)


def _merge_kernel(x_ref, oa_ref, ob_ref, zb_ref, g_ref, bg_ref, wa_ref, wb_ref, wo_ref, out_ref):
    zb = zb_ref[...].astype(F32)
    ob = (ob_ref[...].astype(F32) * _silu(zb)).astype(BF16)
    ya = jnp.dot(oa_ref[...], wa_ref[...], preferred_element_type=F32)
    yb = jnp.dot(ob, wb_ref[...], preferred_element_type=F32)
    gates = jax.nn.sigmoid(g_ref[...].astype(F32) + bg_ref[...])
    merged = gates[:, :D_MODEL] * ya + gates[:, D_MODEL:] * yb
    out_ref[...] = x_ref[...] + jnp.dot(merged.astype(BF16), wo_ref[...], preferred_element_type=F32)


def _merge(x2, o_a, o_b, proj, b_gate, w_up_a, w_up_b, w_out):
    t = x2.shape[0]
    zb_col = (A_QKV + DN_WIDTH + 3 * SB_WIDTH) // SB_WIDTH
    g_col = (A_QKV + DN_WIDTH + 4 * SB_WIDTH) // (2 * D_MODEL)
    const = lambda i: (0, 0)
    return pl.pallas_call(
        _merge_kernel,
        grid=(t // OUT_TM,),
        in_specs=[
            pl.BlockSpec((OUT_TM, D_MODEL), lambda i: (i, 0)),
            pl.BlockSpec((OUT_TM, DN_WIDTH), lambda i: (i, 0)),
            pl.BlockSpec((OUT_TM, SB_WIDTH), lambda i: (i, 0)),
            pl.BlockSpec((OUT_TM, SB_WIDTH), lambda i: (i, zb_col)),
            pl.BlockSpec((OUT_TM, 2 * D_MODEL), lambda i: (i, g_col)),
            pl.BlockSpec((1, 2 * D_MODEL), const),
            pl.BlockSpec((DN_WIDTH, D_MODEL), const),
            pl.BlockSpec((SB_WIDTH, D_MODEL), const),
            pl.BlockSpec((D_MODEL, D_MODEL), const),
        ],
        out_specs=pl.BlockSpec((OUT_TM, D_MODEL), lambda i: (i, 0)),
        out_shape=jax.ShapeDtypeStruct((t, D_MODEL), F32),
        compiler_params=pltpu.CompilerParams(
            dimension_semantics=("arbitrary",), vmem_limit_bytes=VMEM_LIMIT),
        name="merge",
    )(x2, o_a, o_b, proj, proj, b_gate, w_up_a, w_up_b, w_out)


def _layer(x, norm_gain, w_in, b_gate, conv_w, a_log, dt_bias, dn_out_gain,
           sb_q_gain, sb_k_gain, w_up_a, w_up_b, w_out):
    batch, seq, _ = x.shape
    x2 = x.reshape(batch * seq, D_MODEL)

    w_main = jnp.concatenate([w_in[:, :BA_COL0], w_in[:, BA_COL0 + BA_COLS:]], axis=1).astype(BF16)
    w_ba = jnp.pad(w_in[:, BA_COL0:BA_COL0 + BA_COLS], ((0, 0), (0, LANES - BA_COLS)))
    wba_hi = w_ba.astype(BF16)
    wba_lo = (w_ba - wba_hi.astype(F32)).astype(BF16)
    par = jnp.zeros((8, LANES), F32)
    par = par.at[0, DN_HEADS:2 * DN_HEADS].set(a_log).at[1, DN_HEADS:2 * DN_HEADS].set(dt_bias)
    qg2 = jnp.concatenate([sb_q_gain, sb_q_gain])[None, :]
    kg2 = jnp.concatenate([sb_k_gain, sb_k_gain])[None, :]
    kk = jnp.arange(SB_BLOCK)
    cum1 = jnp.concatenate([(kk[:, None] > kk[None, :]).astype(BF16), jnp.ones((SB_BLOCK, SB_BLOCK), BF16)], axis=1)
    cum_mat = jnp.concatenate([cum1, cum1], axis=0)
    ln = jnp.arange(LANES)
    half1 = ((ln[:, None] // SB_DH) == (ln[None, :] // SB_DH)).astype(BF16)
    half_ones2 = jnp.concatenate([half1, half1], axis=0)
    tt = jnp.arange(DN_TS)
    ltri1 = ((tt[:, None] // CHUNK == tt[None, :] // CHUNK) & (tt[:, None] >= tt[None, :])).astype(BF16)
    ltri3 = jnp.concatenate([ltri1, ltri1, ltri1], axis=1)

    proj, ba = _inproj(x2, norm_gain[None, :], w_main, wba_hi, wba_lo)
    o_a = _deltanet(proj, ba, conv_w, par, dn_out_gain[None, :], ltri3, batch, seq)
    o_b = _stick_breaking(proj, qg2, kg2, cum_mat, half_ones2, batch, seq)
    out = _merge(x2, o_a, o_b, proj, b_gate[None, :], w_up_a.astype(BF16), w_up_b.astype(BF16),
                 w_out.astype(BF16))
    return out.reshape(batch, seq, D_MODEL)


def kernel(x, norm_gain, w_in, b_gate, conv_w, a_log, dt_bias, dn_out_gain, sb_q_gain, sb_k_gain,
           w_up_a, w_up_b, w_out):
    h = x
    for layer in range(norm_gain.shape[0]):
        h = _layer(h, norm_gain[layer], w_in[layer], b_gate[layer], conv_w[layer], a_log[layer],
                   dt_bias[layer], dn_out_gain[layer], sb_q_gain[layer], sb_k_gain[layer],
                   w_up_a[layer], w_up_b[layer], w_out[layer])
    return h
```

```python
import functools

import jax
import jax.numpy as jnp
from jax import lax
from jax.experimental import pallas as pl
from jax.experimental.pallas import tpu as pltpu

F32 = jnp.float32
BF16 = jnp.bfloat16

D_MODEL = 1024
CHUNK = 64
SB_BLOCK = 128
EPS = 1e-6
LOG2E = 1.4426950408889634
ZERO_WEIGHT_LOG2 = -151.0
DN_HEADS = 4
DN_DK = 128
DN_DV = 128
DN_CONV = 4
SB_HEADS = 8
SB_DH = 64
DN_WIDTH = DN_HEADS * DN_DV
SB_WIDTH = SB_HEADS * SB_DH
A_QKV = 2 * DN_HEADS * DN_DK + DN_WIDTH
BA_COL0 = A_QKV + DN_WIDTH
BA_COLS = 2 * DN_HEADS
MAIN_WIDTH = A_QKV + DN_WIDTH + 3 * SB_WIDTH + SB_WIDTH + 2 * D_MODEL

LANES = 128
PAIR = 2 * CHUNK

IN_TM, IN_TN = 1024, 1024
DN_TS = 256
OUT_TM = 512

VMEM_LIMIT = 48 * 1024 * 1024


def _split3(x):
    hi = x.astype(BF16)
    r1 = x - hi.astype(F32)
    mid = r1.astype(BF16)
    lo = (r1 - mid.astype(F32)).astype(BF16)
    return hi, mid, lo


def _split2(x):
    hi = x.astype(BF16)
    lo = (x - hi.astype(F32)).astype(BF16)
    return hi, lo


def _silu(x):
    h = 0.5 * x
    return h + h * jnp.tanh(h)


def _inproj_kernel(x_ref, gain_ref, w_ref, wba_hi_ref, wba_lo_ref, proj_ref, ba_ref, xn_ref):
    @pl.when(pl.program_id(1) == 0)
    def _():
        x = x_ref[...]
        xn = x * lax.rsqrt(jnp.mean(x * x, axis=-1, keepdims=True) + EPS) * gain_ref[...]
        hi, lo = _split2(xn)
        xn_ref[...] = hi
        ba_ref[...] = (jnp.dot(hi, wba_hi_ref[...], preferred_element_type=F32)
                       + jnp.dot(lo, wba_hi_ref[...], preferred_element_type=F32)
                       + jnp.dot(hi, wba_lo_ref[...], preferred_element_type=F32))

    proj_ref[...] = jnp.dot(xn_ref[...], w_ref[...], preferred_element_type=F32).astype(proj_ref.dtype)


def _inproj(x2, gain, w_main, wba_hi, wba_lo):
    t = x2.shape[0]
    return pl.pallas_call(
        _inproj_kernel,
        grid=(t // IN_TM, MAIN_WIDTH // IN_TN),
        in_specs=[
            pl.BlockSpec((IN_TM, D_MODEL), lambda i, j: (i, 0)),
            pl.BlockSpec((1, D_MODEL), lambda i, j: (0, 0)),
            pl.BlockSpec((D_MODEL, IN_TN), lambda i, j: (0, j)),
            pl.BlockSpec((D_MODEL, LANES), lambda i, j: (0, 0)),
            pl.BlockSpec((D_MODEL, LANES), lambda i, j: (0, 0)),
        ],
        out_specs=[
            pl.BlockSpec((IN_TM, IN_TN), lambda i, j: (i, j)),
            pl.BlockSpec((IN_TM, LANES), lambda i, j: (i, 0)),
        ],
        out_shape=[
            jax.ShapeDtypeStruct((t, MAIN_WIDTH), BF16),
            jax.ShapeDtypeStruct((t, LANES), F32),
        ],
        scratch_shapes=[pltpu.VMEM((IN_TM, D_MODEL), BF16)],
        compiler_params=pltpu.CompilerParams(
            dimension_semantics=("arbitrary", "arbitrary"), vmem_limit_bytes=VMEM_LIMIT),
        name="inproj",
    )(x2, gain, w_main, wba_hi, wba_lo)


def _tri_inverse(ms, row, col):
    blk8 = (row >> 3) == (col >> 3)
    eye = jnp.where(row == col, 1.0, 0.0)
    pds = [jnp.where(blk8, -m, 0.0) for m in ms]
    pdb = [pd.astype(BF16) for pd in pds]
    xs = [eye + pd for pd in pds]
    p2b = [jnp.dot(b, b, preferred_element_type=F32).astype(BF16) for b in pdb]
    xs = [x + jnp.dot(x.astype(BF16), p2, preferred_element_type=F32) for x, p2 in zip(xs, p2b)]
    p4b = [jnp.dot(b, b, preferred_element_type=F32).astype(BF16) for b in p2b]
    xs = [x + jnp.dot(x.astype(BF16), p4, preferred_element_type=F32) for x, p4 in zip(xs, p4b)]
    for s in (8, 16, 32):
        sh = s.bit_length()
        off = ((row >> sh) == (col >> sh)) & ((row & (2 * s - 1)) >= s) & ((col & (2 * s - 1)) < s)
        xb = [x.astype(BF16) for x in xs]
        ts_ = [jnp.dot(b, jnp.where(off, m, 0.0).astype(BF16), preferred_element_type=F32).astype(BF16)
               for b, m in zip(xb, ms)]
        xs = [x - jnp.dot(t, b, preferred_element_type=F32) for x, t, b in zip(xs, ts_, xb)]
    return xs


def _deltanet_kernel(qkv_ref, z_ref, ba_ref, convw_ref, par_ref, gain_ref, ltri_ref, o_ref,
                     buf_ref, act_ref, state_ref, ol_ref, qt_ref, mc_ref, bc_ref, ge_ref):
    ts = qkv_ref.shape[0]
    nqk = DN_HEADS * DN_DK
    heads = range(DN_HEADS)

    @pl.when(pl.program_id(1) == 0)
    def _():
        buf_ref[:, 0:8, :] = jnp.zeros((A_QKV // LANES, 8, LANES), F32)
        state_ref[...] = jnp.zeros_like(state_ref)

    for s in range(A_QKV // LANES):
        cs = slice(s * LANES, (s + 1) * LANES)
        buf_ref[s, 8:8 + ts, :] = qkv_ref[:, cs].astype(F32)
        acc = None
        for i in range(DN_CONV):
            lo = 8 - (DN_CONV - 1) + i
            term = buf_ref[s, lo:lo + ts, :] * convw_ref[i:i + 1, cs]
            acc = term if acc is None else acc + term
        y = _silu(acc)
        if s < 2 * DN_HEADS:
            y = y * lax.rsqrt(jnp.sum(y * y, axis=-1, keepdims=True) + EPS)
        if s < DN_HEADS:
            y = y * (DN_DK ** -0.5)
        act_ref[:, cs] = y
        buf_ref[s, 0:8, :] = buf_ref[s, ts:ts + 8, :]

    ba = ba_ref[...]
    beta_all = jax.nn.sigmoid(ba)
    xs = ba + par_ref[1:2, :]
    softplus = jnp.maximum(xs, 0.0) + jnp.log(1.0 + jnp.exp(-jnp.abs(xs)))
    g_all = -jnp.exp(par_ref[0:1, :]) * softplus
    gc_all = jnp.dot(ltri_ref[...], jnp.concatenate(_split3(g_all), axis=0), preferred_element_type=F32)

    row = lax.broadcasted_iota(jnp.int32, (PAIR, PAIR), 0)
    col = lax.broadcasted_iota(jnp.int32, (PAIR, PAIR), 1)
    same = (row >> 6) == (col >> 6)
    lower = same & (row >= col)
    strict = same & (row > col)
    rcol = lax.broadcasted_iota(jnp.int32, (PAIR, 1), 0)

    units = [(p, h) for p in range(ts // PAIR) for h in heads]
    rows_of = [slice(p * PAIR, (p + 1) * PAIR) for p, _ in units]
    gc_t = [gc_all[p * PAIR:(p + 1) * PAIR, :].T for p in range(ts // PAIR)]
    q = [act_ref[rs, h * DN_DK:(h + 1) * DN_DK] for rs, (_, h) in zip(rows_of, units)]
    k = [act_ref[rs, nqk + h * DN_DK:nqk + (h + 1) * DN_DK] for rs, (_, h) in zip(rows_of, units)]
    v = [act_ref[rs, 2 * nqk + h * DN_DV:2 * nqk + (h + 1) * DN_DV] for rs, (_, h) in zip(rows_of, units)]
    beta = [beta_all[rs, h:h + 1] for rs, (_, h) in zip(rows_of, units)]
    gcc = [gc_all[rs, DN_HEADS + h:DN_HEADS + h + 1] for rs, (_, h) in zip(rows_of, units)]
    decay = [jnp.where(lower, jnp.exp(g - gc_t[p][DN_HEADS + h:DN_HEADS + h + 1, :]), 0.0)
             for g, (p, h) in zip(gcc, units)]
    eg = [jnp.exp(g) for g in gcc]
    gend = [jnp.where(rcol < CHUNK, g[CHUNK - 1:CHUNK, :], g[PAIR - 1:PAIR, :]) for g in gcc]
    kb = [x.astype(BF16) for x in k]
    qkk = [lax.dot_general(jnp.concatenate([qq.astype(BF16), kk], axis=0), kk,
                           (((1,), (1,)), ((), ())), preferred_element_type=F32) for qq, kk in zip(q, kb)]
    ms = [jnp.where(strict, b * x[PAIR:] * d, 0.0) for b, x, d in zip(beta, qkk, decay)]
    tinv = _tri_inverse(ms, row, col)
    sol = [jnp.dot(t.astype(BF16),
                   jnp.concatenate([(b * vv).astype(BF16), (b * e * kk).astype(BF16)], axis=1),
                   preferred_element_type=F32) for t, b, vv, e, kk in zip(tinv, beta, v, eg, k)]
    solb = [x.astype(BF16) for x in sol]
    qkb = [jnp.where(lower, x[:PAIR] * d, 0.0).astype(BF16) for x, d in zip(qkk, decay)]
    qwu = [jnp.dot(a, b, preferred_element_type=F32) for a, b in zip(qkb, solb)]
    kd = [kk * jnp.exp(ge - g) for kk, ge, g in zip(k, gend, gcc)]
    for c in range(PAIR // CHUNK):
        kdc = [jnp.where((row >> 6) == c, x, 0.0).astype(BF16) for x in kd]
        kwu = [lax.dot_general(a, b, (((0,), (0,)), ((), ())), preferred_element_type=F32)
               for a, b in zip(kdc, solb)]
        for (p, h), x, g in zip(units, kwu, gcc):
            cc = p * (PAIR // CHUNK) + c
            bc_ref[h, cc] = x[:, :DN_DV]
            mc_ref[h, cc] = x[:, DN_DV:].astype(BF16)
            ge_ref[h, cc] = jnp.broadcast_to(jnp.exp(g[(c + 1) * CHUNK - 1:(c + 1) * CHUNK, :]), (8, LANES))
    for (p, h), rs, x, qq, e in zip(units, rows_of, qwu, q, eg):
        ol_ref[h, rs, :] = x[:, :DN_DV]
        qt_ref[h, rs, :] = (qq * e - x[:, DN_DV:]).astype(BF16)


    gain = gain_ref[...]
    st = [state_ref[h] for h in heads]
    for cc in range(ts // CHUNK):
        cr = slice(cc * CHUNK, (cc + 1) * CHUNK)
        sb = [x.astype(BF16) for x in st]
        o = [ol_ref[h, cr, :] + jnp.dot(qt_ref[h, cr, :], sb[h], preferred_element_type=F32) for h in heads]
        st = [st[h] * ge_ref[h, cc, 0:1, :] + bc_ref[h, cc]
              - jnp.dot(mc_ref[h, cc], sb[h], preferred_element_type=F32) for h in heads]
        for h in heads:
            hs = slice(h * DN_DV, (h + 1) * DN_DV)
            on = o[h] * lax.rsqrt(jnp.mean(o[h] * o[h], axis=-1, keepdims=True) + EPS) * gain
            zz = z_ref[cr, hs].astype(F32)
            o_ref[cr, hs] = (on * _silu(zz)).astype(o_ref.dtype)
    for h in heads:
        state_ref[h] = st[h]


def _deltanet(proj, ba, conv_w, par, gain, ltri3, batch, seq):
    nsb = seq // DN_TS
    nchunk = DN_TS // CHUNK
    return pl.pallas_call(
        _deltanet_kernel,
        grid=(batch, nsb),
        in_specs=[
            pl.BlockSpec((DN_TS, A_QKV), lambda b, s: (b * nsb + s, 0)),
            pl.BlockSpec((DN_TS, DN_WIDTH), lambda b, s: (b * nsb + s, A_QKV // DN_WIDTH)),
            pl.BlockSpec((DN_TS, LANES), lambda b, s: (b * nsb + s, 0)),
            pl.BlockSpec((DN_CONV, A_QKV), lambda b, s: (0, 0)),
            pl.BlockSpec((8, LANES), lambda b, s: (0, 0)),
            pl.BlockSpec((1, DN_DV), lambda b, s: (0, 0)),
            pl.BlockSpec((DN_TS, 3 * DN_TS), lambda b, s: (0, 0)),
        ],
        out_specs=pl.BlockSpec((DN_TS, DN_WIDTH), lambda b, s: (b * nsb + s, 0)),
        out_shape=jax.ShapeDtypeStruct((batch * seq, DN_WIDTH), BF16),
        scratch_shapes=[
            pltpu.VMEM((A_QKV // LANES, DN_TS + 8, LANES), F32),
            pltpu.VMEM((DN_TS, A_QKV), F32),
            pltpu.VMEM((DN_HEADS, DN_DK, DN_DV), F32),
            pltpu.VMEM((DN_HEADS, DN_TS, DN_DV), F32),
            pltpu.VMEM((DN_HEADS, DN_TS, DN_DK), BF16),
            pltpu.VMEM((DN_HEADS, nchunk, DN_DK, DN_DK), BF16),
            pltpu.VMEM((DN_HEADS, nchunk, DN_DK, DN_DV), F32),
            pltpu.VMEM((DN_HEADS, nchunk, 8, LANES), F32),
        ],
        compiler_params=pltpu.CompilerParams(
            dimension_semantics=("arbitrary", "arbitrary"), vmem_limit_bytes=VMEM_LIMIT),
        name="deltanet",
    )(proj, proj, ba, conv_w, par, gain, ltri3)


def _rms_halves(x, half_ones2):
    hi, lo = _split2(x * x)
    ss = jnp.dot(jnp.concatenate([hi, lo], axis=1), half_ones2, preferred_element_type=F32)
    return x * lax.rsqrt(ss * (1.0 / SB_DH) + EPS)


def _sb_kernel(q_ref, k_ref, v_ref, qg_ref, kg_ref, cum_ref, half_ref, o_ref,
               kn_ref, v2_ref, r_ref, acc_ref):
    i = pl.program_id(1)
    nblk = k_ref.shape[0] // SB_BLOCK
    npair = q_ref.shape[1] // LANES
    pairs = range(npair)
    lane = lax.broadcasted_iota(jnp.int32, (SB_BLOCK, LANES), 1)
    first = lane < SB_DH

    def cols(p):
        return slice(p * LANES, (p + 1) * LANES)

    @pl.when(i == 0)
    def _():
        def prep(j, carry):
            rows = pl.ds(pl.multiple_of(j * SB_BLOCK, SB_BLOCK), SB_BLOCK)
            for p in pairs:
                kn = _rms_halves(k_ref[rows, cols(p)].astype(F32), half_ref[...]) * kg_ref[...]
                kn_ref[rows, cols(p)] = kn.astype(BF16)
                vb = v_ref[rows, cols(p)]
                zero = jnp.zeros_like(vb)
                v2_ref[p, j, 0:SB_BLOCK, :] = jnp.where(first, vb, zero)
                v2_ref[p, j, SB_BLOCK:2 * SB_BLOCK, :] = jnp.where(first, zero, vb)
            return carry
        lax.fori_loop(0, nblk, prep, 0)

    qscale = qg_ref[...] * (SB_DH ** -0.5 * LOG2E)
    q2 = []
    for p in pairs:
        qn = _rms_halves(q_ref[:, cols(p)].astype(F32), half_ref[...]) * qscale
        q2.append(jnp.concatenate([jnp.where(first, qn, 0.0), jnp.where(first, 0.0, qn)], axis=0).astype(BF16))

    def scores(rows, mask):
        zs = [lax.dot_general(q2[p], kn_ref[rows, cols(p)], (((1,), (1,)), ((), ())),
                              preferred_element_type=F32) for p in pairs]
        lss = [jnp.minimum(z, 0.0) - jnp.log(1.0 + jnp.exp2(-jnp.abs(z))) * LOG2E for z in zs]
        lks = [ls - z for ls, z in zip(lss, zs)]
        if mask is not None:
            lks = [jnp.where(mask, lk, 0.0) for lk in lks]
        return lss, [jnp.concatenate(_split2(lk), axis=1) for lk in lks]

    def cumsums(lk2s):
        return [jnp.dot(lk2, cum_ref[...], preferred_element_type=F32) for lk2 in lk2s]

    def weights(lss, css, rs, mask):
        avs = [jnp.exp2(ls + cs[:, :SB_BLOCK] + r) for ls, cs, r in zip(lss, css, rs)]
        if mask is not None:
            avs = [jnp.where(mask, a, 0.0) for a in avs]
        out = []
        for a in avs:
            ab = a.astype(BF16)
            out.append(jnp.concatenate([ab[:SB_BLOCK], ab[SB_BLOCK:]], axis=1))
        return out

    def one_tile(rows, j, rs, accs, mask):
        lss, lk2s = scores(rows, mask)
        css = cumsums(lk2s)
        a2s = weights(lss, css, rs, mask)
        rs = [r + cs[:, SB_BLOCK:] for r, cs in zip(rs, css)]
        accs = [acc + jnp.dot(a2, v2_ref[p, j], preferred_element_type=F32)
                for p, (acc, a2) in enumerate(zip(accs, a2s))]
        return rs, accs

    def load_state():
        return [r_ref[p] for p in pairs], [acc_ref[:, cols(p)] for p in pairs]

    def store_state(rs, accs):
        for p in pairs:
            r_ref[p] = rs[p]
            acc_ref[:, cols(p)] = accs[p]

    def tile_rows(j):
        return pl.ds(pl.multiple_of(j * SB_BLOCK, SB_BLOCK), SB_BLOCK)

    qrow = lax.broadcasted_iota(jnp.int32, (2 * SB_BLOCK, SB_BLOCK), 0) & (SB_BLOCK - 1)
    kcol = lax.broadcasted_iota(jnp.int32, (2 * SB_BLOCK, SB_BLOCK), 1)
    store_state(*one_tile(tile_rows(i), i,
                          [jnp.zeros((2 * SB_BLOCK, SB_BLOCK), F32)] * npair,
                          [jnp.zeros((SB_BLOCK, LANES), F32)] * npair, kcol < qrow))

    def cond(carry):
        t, alive = carry
        return jnp.logical_and(t < i, alive > 0)

    def body(carry):
        t, _ = carry
        j = i - 1 - t
        rs, accs = one_tile(tile_rows(j), j, *load_state(), None)
        store_state(rs, accs)
        top = functools.reduce(jnp.maximum, rs)
        return t + 1, (jnp.max(top) >= ZERO_WEIGHT_LOG2).astype(jnp.int32)
    lax.while_loop(cond, body, (jnp.int32(0), jnp.int32(1)))

    o_ref[...] = acc_ref[...].astype(o_ref.dtype)


def _stick_breaking(proj, qg2, kg2, cum_mat, half_ones2, batch, seq):
    nq = seq // SB_BLOCK
    npair = SB_HEADS // 2
    qcol = (A_QKV + DN_WIDTH) // SB_WIDTH
    return pl.pallas_call(
        _sb_kernel,
        grid=(batch, nq),
        in_specs=[
            pl.BlockSpec((SB_BLOCK, SB_WIDTH), lambda b, i: (b * nq + i, qcol)),
            pl.BlockSpec((seq, SB_WIDTH), lambda b, i: (b, qcol + 1)),
            pl.BlockSpec((seq, SB_WIDTH), lambda b, i: (b, qcol + 2)),
            pl.BlockSpec((1, LANES), lambda b, i: (0, 0)),
            pl.BlockSpec((1, LANES), lambda b, i: (0, 0)),
            pl.BlockSpec((2 * SB_BLOCK, 2 * SB_BLOCK), lambda b, i: (0, 0)),
            pl.BlockSpec((2 * LANES, LANES), lambda b, i: (0, 0)),
        ],
        out_specs=pl.BlockSpec((SB_BLOCK, SB_WIDTH), lambda b, i: (b * nq + i, 0)),
        out_shape=jax.ShapeDtypeStruct((batch * seq, SB_WIDTH), BF16),
        scratch_shapes=[
            pltpu.VMEM((seq, SB_WIDTH), BF16),
            pltpu.VMEM((npair, nq, 2 * SB_BLOCK, LANES), BF16),
            pltpu.VMEM((npair, 2 * SB_BLOCK, SB_BLOCK), F32),
            pltpu.VMEM((SB_BLOCK, SB_WIDTH), F32),
        ],
        compiler_params=pltpu.CompilerParams(
            dimension_semantics=("arbitrary", "arbitrary"), vmem_limit_bytes=VMEM_LIMIT),
        name="stickbreak",
    )(proj, proj, proj, qg2, kg2, cum_mat, half_ones2)


def _merge_kernel(x_ref, oa_ref, ob_ref, zb_ref, g_ref, bg_ref, wa_ref, wb_ref, wo_ref, out_ref):
    zb = zb_ref[...].astype(F32)
    ob = (ob_ref[...].astype(F32) * _silu(zb)).astype(BF16)
    ya = jnp.dot(oa_ref[...], wa_ref[...], preferred_element_type=F32)
    yb = jnp.dot(ob, wb_ref[...], preferred_element_type=F32)
    gates = jax.nn.sigmoid(g_ref[...].astype(F32) + bg_ref[...])
    merged = gates[:, :D_MODEL] * ya + gates[:, D_MODEL:] * yb
    out_ref[...] = x_ref[...] + jnp.dot(merged.astype(BF16), wo_ref[...], preferred_element_type=F32)


def _merge(x2, o_a, o_b, proj, b_gate, w_up_a, w_up_b, w_out):
    t = x2.shape[0]
    zb_col = (A_QKV + DN_WIDTH + 3 * SB_WIDTH) // SB_WIDTH
    g_col = (A_QKV + DN_WIDTH + 4 * SB_WIDTH) // (2 * D_MODEL)
    const = lambda i: (0, 0)
    return pl.pallas_call(
        _merge_kernel,
        grid=(t // OUT_TM,),
        in_specs=[
            pl.BlockSpec((OUT_TM, D_MODEL), lambda i: (i, 0)),
            pl.BlockSpec((OUT_TM, DN_WIDTH), lambda i: (i, 0)),
            pl.BlockSpec((OUT_TM, SB_WIDTH), lambda i: (i, 0)),
            pl.BlockSpec((OUT_TM, SB_WIDTH), lambda i: (i, zb_col)),
            pl.BlockSpec((OUT_TM, 2 * D_MODEL), lambda i: (i, g_col)),
            pl.BlockSpec((1, 2 * D_MODEL), const),
            pl.BlockSpec((DN_WIDTH, D_MODEL), const),
            pl.BlockSpec((SB_WIDTH, D_MODEL), const),
            pl.BlockSpec((D_MODEL, D_MODEL), const),
        ],
        out_specs=pl.BlockSpec((OUT_TM, D_MODEL), lambda i: (i, 0)),
        out_shape=jax.ShapeDtypeStruct((t, D_MODEL), F32),
        compiler_params=pltpu.CompilerParams(
            dimension_semantics=("arbitrary",), vmem_limit_bytes=VMEM_LIMIT),
        name="merge",
    )(x2, o_a, o_b, proj, proj, b_gate, w_up_a, w_up_b, w_out)


def _layer(x, norm_gain, w_in, b_gate, conv_w, a_log, dt_bias, dn_out_gain,
           sb_q_gain, sb_k_gain, w_up_a, w_up_b, w_out):
    batch, seq, _ = x.shape
    x2 = x.reshape(batch * seq, D_MODEL)

    w_main = jnp.concatenate([w_in[:, :BA_COL0], w_in[:, BA_COL0 + BA_COLS:]], axis=1).astype(BF16)
    w_ba = jnp.pad(w_in[:, BA_COL0:BA_COL0 + BA_COLS], ((0, 0), (0, LANES - BA_COLS)))
    wba_hi = w_ba.astype(BF16)
    wba_lo = (w_ba - wba_hi.astype(F32)).astype(BF16)
    par = jnp.zeros((8, LANES), F32)
    par = par.at[0, DN_HEADS:2 * DN_HEADS].set(a_log).at[1, DN_HEADS:2 * DN_HEADS].set(dt_bias)
    qg2 = jnp.concatenate([sb_q_gain, sb_q_gain])[None, :]
    kg2 = jnp.concatenate([sb_k_gain, sb_k_gain])[None, :]
    kk = jnp.arange(SB_BLOCK)
    cum1 = jnp.concatenate([(kk[:, None] > kk[None, :]).astype(BF16), jnp.ones((SB_BLOCK, SB_BLOCK), BF16)], axis=1)
    cum_mat = jnp.concatenate([cum1, cum1], axis=0)
    ln = jnp.arange(LANES)
    half1 = ((ln[:, None] // SB_DH) == (ln[None, :] // SB_DH)).astype(BF16)
    half_ones2 = jnp.concatenate([half1, half1], axis=0)
    tt = jnp.arange(DN_TS)
    ltri1 = ((tt[:, None] // CHUNK == tt[None, :] // CHUNK) & (tt[:, None] >= tt[None, :])).astype(BF16)
    ltri3 = jnp.concatenate([ltri1, ltri1, ltri1], axis=1)

    proj, ba = _inproj(x2, norm_gain[None, :], w_main, wba_hi, wba_lo)
    o_a = _deltanet(proj, ba, conv_w, par, dn_out_gain[None, :], ltri3, batch, seq)
    o_b = _stick_breaking(proj, qg2, kg2, cum_mat, half_ones2, batch, seq)
    out = _merge(x2, o_a, o_b, proj, b_gate[None, :], w_up_a.astype(BF16), w_up_b.astype(BF16),
                 w_out.astype(BF16))
    return out.reshape(batch, seq, D_MODEL)


def kernel(x, norm_gain, w_in, b_gate, conv_w, a_log, dt_bias, dn_out_gain, sb_q_gain, sb_k_gain,
           w_up_a, w_up_b, w_out):
    h = x
    for layer in range(norm_gain.shape[0]):
        h = _layer(h, norm_gain[layer], w_in[layer], b_gate[layer], conv_w[layer], a_log[layer],
                   dt_bias[layer], dn_out_gain[layer], sb_q_gain[layer], sb_k_gain[layer],
                   w_up_a[layer], w_up_b[layer], w_out[layer])
    return h
```

```python
import functools

import jax
import jax.numpy as jnp
from jax import lax
from jax.experimental import pallas as pl
from jax.experimental.pallas import tpu as pltpu

F32 = jnp.float32
BF16 = jnp.bfloat16

D_MODEL = 1024
CHUNK = 64
SB_BLOCK = 128
EPS = 1e-6
LOG2E = 1.4426950408889634
ZERO_WEIGHT_LOG2 = -151.0
DN_HEADS = 4
DN_DK = 128
DN_DV = 128
DN_CONV = 4
SB_HEADS = 8
SB_DH = 64
DN_WIDTH = DN_HEADS * DN_DV
SB_WIDTH = SB_HEADS * SB_DH
A_QKV = 2 * DN_HEADS * DN_DK + DN_WIDTH
BA_COL0 = A_QKV + DN_WIDTH
BA_COLS = 2 * DN_HEADS
MAIN_WIDTH = A_QKV + DN_WIDTH + 3 * SB_WIDTH + SB_WIDTH + 2 * D_MODEL

LANES = 128
PAIR = 2 * CHUNK

IN_TM, IN_TN = 1024, 1024
DN_TS = 1024
DN_WAVE = 256
OUT_TM = 512

VMEM_LIMIT = 48 * 1024 * 1024


def _split3(x):
    hi = x.astype(BF16)
    r1 = x - hi.astype(F32)
    mid = r1.astype(BF16)
    lo = (r1 - mid.astype(F32)).astype(BF16)
    return hi, mid, lo


def _split2(x):
    hi = x.astype(BF16)
    lo = (x - hi.astype(F32)).astype(BF16)
    return hi, lo


def _silu(x):
    h = 0.5 * x
    return h + h * jnp.tanh(h)


def _inproj_kernel(x_ref, gain_ref, w_ref, wba_hi_ref, wba_lo_ref, proj_ref, ba_ref, xn_ref):
    @pl.when(pl.program_id(1) == 0)
    def _():
        x = x_ref[...]
        xn = x * lax.rsqrt(jnp.mean(x * x, axis=-1, keepdims=True) + EPS) * gain_ref[...]
        hi, lo = _split2(xn)
        xn_ref[...] = hi
        ba_ref[...] = (jnp.dot(hi, wba_hi_ref[...], preferred_element_type=F32)
                       + jnp.dot(lo, wba_hi_ref[...], preferred_element_type=F32)
                       + jnp.dot(hi, wba_lo_ref[...], preferred_element_type=F32))

    proj_ref[...] = jnp.dot(xn_ref[...], w_ref[...], preferred_element_type=F32).astype(proj_ref.dtype)


def _inproj(x2, gain, w_main, wba_hi, wba_lo):
    t = x2.shape[0]
    return pl.pallas_call(
        _inproj_kernel,
        grid=(t // IN_TM, MAIN_WIDTH // IN_TN),
        in_specs=[
            pl.BlockSpec((IN_TM, D_MODEL), lambda i, j: (i, 0)),
            pl.BlockSpec((1, D_MODEL), lambda i, j: (0, 0)),
            pl.BlockSpec((D_MODEL, IN_TN), lambda i, j: (0, j)),
            pl.BlockSpec((D_MODEL, LANES), lambda i, j: (0, 0)),
            pl.BlockSpec((D_MODEL, LANES), lambda i, j: (0, 0)),
        ],
        out_specs=[
            pl.BlockSpec((IN_TM, IN_TN), lambda i, j: (i, j)),
            pl.BlockSpec((IN_TM, LANES), lambda i, j: (i, 0)),
        ],
        out_shape=[
            jax.ShapeDtypeStruct((t, MAIN_WIDTH), BF16),
            jax.ShapeDtypeStruct((t, LANES), F32),
        ],
        scratch_shapes=[pltpu.VMEM((IN_TM, D_MODEL), BF16)],
        compiler_params=pltpu.CompilerParams(
            dimension_semantics=("arbitrary", "arbitrary"), vmem_limit_bytes=VMEM_LIMIT),
        name="inproj",
    )(x2, gain, w_main, wba_hi, wba_lo)


def _tri_inverse(ms, row, col):
    blk8 = (row >> 3) == (col >> 3)
    eye = jnp.where(row == col, 1.0, 0.0)
    pds = [jnp.where(blk8, -m, 0.0) for m in ms]
    pdb = [pd.astype(BF16) for pd in pds]
    xs = [eye + pd for pd in pds]
    p2b = [jnp.dot(b, b, preferred_element_type=F32).astype(BF16) for b in pdb]
    yield
    xs = [x + jnp.dot(x.astype(BF16), p2, preferred_element_type=F32) for x, p2 in zip(xs, p2b)]
    yield
    p4b = [jnp.dot(b, b, preferred_element_type=F32).astype(BF16) for b in p2b]
    yield
    xs = [x + jnp.dot(x.astype(BF16), p4, preferred_element_type=F32) for x, p4 in zip(xs, p4b)]
    yield
    for s in (8, 16, 32):
        sh = s.bit_length()
        off = ((row >> sh) == (col >> sh)) & ((row & (2 * s - 1)) >= s) & ((col & (2 * s - 1)) < s)
        xb = [x.astype(BF16) for x in xs]
        ts_ = [jnp.dot(b, jnp.where(off, m, 0.0).astype(BF16), preferred_element_type=F32).astype(BF16)
               for b, m in zip(xb, ms)]
        yield
        xs = [x - jnp.dot(t, b, preferred_element_type=F32) for x, t, b in zip(xs, ts_, xb)]
        yield
    return xs


def _round_robin(*gens):
    live = list(gens)
    while live:
        for g in list(live):
            try:
                next(g)
            except StopIteration:
                live.remove(g)


def _deltanet_kernel(qkv_ref, z_ref, ba_ref, convw_ref, par_ref, gain_ref, ltri_ref, o_ref,
                     buf_ref, act_ref, state_ref, ol_ref, qt_ref, mc_ref, bc_ref, ge_ref):
    ts = qkv_ref.shape[0]
    nqk = DN_HEADS * DN_DK
    heads = range(DN_HEADS)
    nslab = A_QKV // LANES
    nwave = ts // DN_WAVE

    @pl.when(pl.program_id(1) == 0)
    def _():
        buf_ref[:, 0:8, :] = jnp.zeros((nslab, 8, LANES), F32)
        state_ref[...] = jnp.zeros_like(state_ref)

    for s in range(nslab):
        buf_ref[s, 8:8 + ts, :] = qkv_ref[:, s * LANES:(s + 1) * LANES].astype(F32)

    def activations(w):
        r0 = w * DN_WAVE
        for s in range(nslab):
            cs = slice(s * LANES, (s + 1) * LANES)
            acc = None
            for i in range(DN_CONV):
                lo = 8 - (DN_CONV - 1) + i + r0
                term = buf_ref[s, lo:lo + DN_WAVE, :] * convw_ref[i:i + 1, cs]
                acc = term if acc is None else acc + term
            y = _silu(acc)
            if s < 2 * DN_HEADS:
                y = y * lax.rsqrt(jnp.sum(y * y, axis=-1, keepdims=True) + EPS)
            if s < DN_HEADS:
                y = y * (DN_DK ** -0.5)
            act_ref[r0:r0 + DN_WAVE, cs] = y
            yield

    ba = ba_ref[...]
    beta_all = jax.nn.sigmoid(ba)
    xs = ba + par_ref[1:2, :]
    softplus = jnp.maximum(xs, 0.0) + jnp.log(1.0 + jnp.exp(-jnp.abs(xs)))
    g_all = -jnp.exp(par_ref[0:1, :]) * softplus
    gc_of = [jnp.dot(ltri_ref[...], jnp.concatenate(_split3(g_all[p * PAIR:(p + 1) * PAIR, :]), axis=0),
                     preferred_element_type=F32) for p in range(ts // PAIR)]

    row = lax.broadcasted_iota(jnp.int32, (PAIR, PAIR), 0)
    col = lax.broadcasted_iota(jnp.int32, (PAIR, PAIR), 1)
    same = (row >> 6) == (col >> 6)
    lower = same & (row >= col)
    strict = same & (row > col)
    rcol = lax.broadcasted_iota(jnp.int32, (PAIR, 1), 0)

    def chunk_local(w):
        pairs = range(w * (DN_WAVE // PAIR), (w + 1) * (DN_WAVE // PAIR))
        units = [(p, h) for p in pairs for h in heads]
        rows_of = [slice(p * PAIR, (p + 1) * PAIR) for p, _ in units]
        gc_t = {p: gc_of[p].T for p in pairs}
        q = [act_ref[rs, h * DN_DK:(h + 1) * DN_DK] for rs, (_, h) in zip(rows_of, units)]
        k = [act_ref[rs, nqk + h * DN_DK:nqk + (h + 1) * DN_DK] for rs, (_, h) in zip(rows_of, units)]
        v = [act_ref[rs, 2 * nqk + h * DN_DV:2 * nqk + (h + 1) * DN_DV] for rs, (_, h) in zip(rows_of, units)]
        beta = [beta_all[rs, h:h + 1] for rs, (_, h) in zip(rows_of, units)]
        gcc = [gc_of[p][:, DN_HEADS + h:DN_HEADS + h + 1] for p, h in units]
        decay = [jnp.where(lower, jnp.exp(g - gc_t[p][DN_HEADS + h:DN_HEADS + h + 1, :]), 0.0)
                 for g, (p, h) in zip(gcc, units)]
        eg = [jnp.exp(g) for g in gcc]
        gend = [jnp.where(rcol < CHUNK, g[CHUNK - 1:CHUNK, :], g[PAIR - 1:PAIR, :]) for g in gcc]
        kb = [x.astype(BF16) for x in k]
        yield
        qkk = [lax.dot_general(jnp.concatenate([qq.astype(BF16), kk], axis=0), kk,
                               (((1,), (1,)), ((), ())), preferred_element_type=F32) for qq, kk in zip(q, kb)]
        yield
        ms = [jnp.where(strict, b * x[PAIR:] * d, 0.0) for b, x, d in zip(beta, qkk, decay)]
        tinv = yield from _tri_inverse(ms, row, col)
        sol = [jnp.dot(t.astype(BF16),
                       jnp.concatenate([(b * vv).astype(BF16), (b * e * kk).astype(BF16)], axis=1),
                       preferred_element_type=F32) for t, b, vv, e, kk in zip(tinv, beta, v, eg, k)]
        yield
        solb = [x.astype(BF16) for x in sol]
        qkb = [jnp.where(lower, x[:PAIR] * d, 0.0).astype(BF16) for x, d in zip(qkk, decay)]
        qwu = [jnp.dot(a, b, preferred_element_type=F32) for a, b in zip(qkb, solb)]
        yield
        kd = [kk * jnp.exp(ge - g) for kk, ge, g in zip(k, gend, gcc)]
        for c in range(PAIR // CHUNK):
            kdc = [jnp.where((row >> 6) == c, x, 0.0).astype(BF16) for x in kd]
            kwu = [lax.dot_general(a, b, (((0,), (0,)), ((), ())), preferred_element_type=F32)
                   for a, b in zip(kdc, solb)]
            yield
            for (p, h), x, g in zip(units, kwu, gcc):
                cc = p * (PAIR // CHUNK) + c
                bc_ref[h, cc] = x[:, :DN_DV]
                mc_ref[h, cc] = x[:, DN_DV:].astype(BF16)
                ge_ref[h, cc] = jnp.broadcast_to(jnp.exp(g[(c + 1) * CHUNK - 1:(c + 1) * CHUNK, :]), (8, LANES))
        for (p, h), rs, x, qq, e in zip(units, rows_of, qwu, q, eg):
            ol_ref[h, rs, :] = x[:, :DN_DV]
            qt_ref[h, rs, :] = (qq * e - x[:, DN_DV:]).astype(BF16)
        yield

    gain = gain_ref[...]
    st = [state_ref[h] for h in heads]

    def recurrence(w):
        for cc in range(w * (DN_WAVE // CHUNK), (w + 1) * (DN_WAVE // CHUNK)):
            cr = slice(cc * CHUNK, (cc + 1) * CHUNK)
            sb = [x.astype(BF16) for x in st]
            o = [ol_ref[h, cr, :] + jnp.dot(qt_ref[h, cr, :], sb[h], preferred_element_type=F32) for h in heads]
            new = [st[h] * ge_ref[h, cc, 0:1, :] + bc_ref[h, cc]
                   - jnp.dot(mc_ref[h, cc], sb[h], preferred_element_type=F32) for h in heads]
            for h in heads:
                st[h] = new[h]
            yield
            for h in heads:
                hs = slice(h * DN_DV, (h + 1) * DN_DV)
                on = o[h] * lax.rsqrt(jnp.mean(o[h] * o[h], axis=-1, keepdims=True) + EPS) * gain
                zz = z_ref[cr, hs].astype(F32)
                o_ref[cr, hs] = (on * _silu(zz)).astype(o_ref.dtype)
            yield

    _round_robin(activations(0))
    for w in range(nwave):
        others = []
        if w + 1 < nwave:
            others.append(activations(w + 1))
        if w > 0:
            others.append(recurrence(w - 1))
        _round_robin(chunk_local(w), *others)
    _round_robin(recurrence(nwave - 1))
    for h in heads:
        state_ref[h] = st[h]
    for s in range(nslab):
        buf_ref[s, 0:8, :] = buf_ref[s, ts:ts + 8, :]


def _deltanet(proj, ba, conv_w, par, gain, ltri3, batch, seq):
    nsb = seq // DN_TS
    nchunk = DN_TS // CHUNK
    return pl.pallas_call(
        _deltanet_kernel,
        grid=(batch, nsb),
        in_specs=[
            pl.BlockSpec((DN_TS, A_QKV), lambda b, s: (b * nsb + s, 0)),
            pl.BlockSpec((DN_TS, DN_WIDTH), lambda b, s: (b * nsb + s, A_QKV // DN_WIDTH)),
            pl.BlockSpec((DN_TS, LANES), lambda b, s: (b * nsb + s, 0)),
            pl.BlockSpec((DN_CONV, A_QKV), lambda b, s: (0, 0)),
            pl.BlockSpec((8, LANES), lambda b, s: (0, 0)),
            pl.BlockSpec((1, DN_DV), lambda b, s: (0, 0)),
            pl.BlockSpec((PAIR, 3 * PAIR), lambda b, s: (0, 0)),
        ],
        out_specs=pl.BlockSpec((DN_TS, DN_WIDTH), lambda b, s: (b * nsb + s, 0)),
        out_shape=jax.ShapeDtypeStruct((batch * seq, DN_WIDTH), BF16),
        scratch_shapes=[
            pltpu.VMEM((A_QKV // LANES, DN_TS + 8, LANES), F32),
            pltpu.VMEM((DN_TS, A_QKV), F32),
            pltpu.VMEM((DN_HEADS, DN_DK, DN_DV), F32),
            pltpu.VMEM((DN_HEADS, DN_TS, DN_DV), F32),
            pltpu.VMEM((DN_HEADS, DN_TS, DN_DK), BF16),
            pltpu.VMEM((DN_HEADS, nchunk, DN_DK, DN_DK), BF16),
            pltpu.VMEM((DN_HEADS, nchunk, DN_DK, DN_DV), F32),
            pltpu.VMEM((DN_HEADS, nchunk, 8, LANES), F32),
        ],
        compiler_params=pltpu.CompilerParams(
            dimension_semantics=("arbitrary", "arbitrary"), vmem_limit_bytes=VMEM_LIMIT),
        name="deltanet",
    )(proj, proj, ba, conv_w, par, gain, ltri3)


def _rms_halves(x, half_ones2):
    hi, lo = _split2(x * x)
    ss = jnp.dot(jnp.concatenate([hi, lo], axis=1), half_ones2, preferred_element_type=F32)
    return x * lax.rsqrt(ss * (1.0 / SB_DH) + EPS)


def _sb_kernel(q_ref, k_ref, v_ref, qg_ref, kg_ref, cum_ref, half_ref, o_ref,
               kn_ref, v2_ref, r_ref, acc_ref):
    i = pl.program_id(1)
    nblk = k_ref.shape[0] // SB_BLOCK
    npair = q_ref.shape[1] // LANES
    pairs = range(npair)
    lane = lax.broadcasted_iota(jnp.int32, (SB_BLOCK, LANES), 1)
    first = lane < SB_DH

    def cols(p):
        return slice(p * LANES, (p + 1) * LANES)

    @pl.when(i == 0)
    def _():
        def prep(j, carry):
            rows = pl.ds(pl.multiple_of(j * SB_BLOCK, SB_BLOCK), SB_BLOCK)
            for p in pairs:
                kn = _rms_halves(k_ref[rows, cols(p)].astype(F32), half_ref[...]) * kg_ref[...]
                kn_ref[rows, cols(p)] = kn.astype(BF16)
                vb = v_ref[rows, cols(p)]
                zero = jnp.zeros_like(vb)
                v2_ref[p, j, 0:SB_BLOCK, :] = jnp.where(first, vb, zero)
                v2_ref[p, j, SB_BLOCK:2 * SB_BLOCK, :] = jnp.where(first, zero, vb)
            return carry
        lax.fori_loop(0, nblk, prep, 0)

    qscale = qg_ref[...] * (SB_DH ** -0.5 * LOG2E)
    q2 = []
    for p in pairs:
        qn = _rms_halves(q_ref[:, cols(p)].astype(F32), half_ref[...]) * qscale
        q2.append(jnp.concatenate([jnp.where(first, qn, 0.0), jnp.where(first, 0.0, qn)], axis=0).astype(BF16))

    def scores(rows, mask):
        zs = [lax.dot_general(q2[p], kn_ref[rows, cols(p)], (((1,), (1,)), ((), ())),
                              preferred_element_type=F32) for p in pairs]
        lss = [jnp.minimum(z, 0.0) - jnp.log(1.0 + jnp.exp2(-jnp.abs(z))) * LOG2E for z in zs]
        lks = [ls - z for ls, z in zip(lss, zs)]
        if mask is not None:
            lks = [jnp.where(mask, lk, 0.0) for lk in lks]
        return lss, [jnp.concatenate(_split2(lk), axis=1) for lk in lks]

    def cumsums(lk2s):
        return [jnp.dot(lk2, cum_ref[...], preferred_element_type=F32) for lk2 in lk2s]

    def weights(lss, css, rs, mask):
        avs = [jnp.exp2(ls + cs[:, :SB_BLOCK] + r) for ls, cs, r in zip(lss, css, rs)]
        if mask is not None:
            avs = [jnp.where(mask, a, 0.0) for a in avs]
        out = []
        for a in avs:
            ab = a.astype(BF16)
            out.append(jnp.concatenate([ab[:SB_BLOCK], ab[SB_BLOCK:]], axis=1))
        return out

    def one_tile(rows, j, rs, accs, mask):
        lss, lk2s = scores(rows, mask)
        css = cumsums(lk2s)
        a2s = weights(lss, css, rs, mask)
        rs = [r + cs[:, SB_BLOCK:] for r, cs in zip(rs, css)]
        accs = [acc + jnp.dot(a2, v2_ref[p, j], preferred_element_type=F32)
                for p, (acc, a2) in enumerate(zip(accs, a2s))]
        return rs, accs

    def load_state():
        return [r_ref[p] for p in pairs], [acc_ref[:, cols(p)] for p in pairs]

    def store_state(rs, accs):
        for p in pairs:
            r_ref[p] = rs[p]
            acc_ref[:, cols(p)] = accs[p]

    def tile_rows(j):
        return pl.ds(pl.multiple_of(j * SB_BLOCK, SB_BLOCK), SB_BLOCK)

    qrow = lax.broadcasted_iota(jnp.int32, (2 * SB_BLOCK, SB_BLOCK), 0) & (SB_BLOCK - 1)
    kcol = lax.broadcasted_iota(jnp.int32, (2 * SB_BLOCK, SB_BLOCK), 1)
    causal = kcol < qrow
    zero_rs = [jnp.zeros((2 * SB_BLOCK, SB_BLOCK), F32)] * npair
    zero_accs = [jnp.zeros((SB_BLOCK, LANES), F32)] * npair

    def alive_flag(rs):
        return (jnp.max(functools.reduce(jnp.maximum, rs)) >= ZERO_WEIGHT_LOG2).astype(jnp.int32)

    @pl.when(i == 0)
    def _():
        store_state(*one_tile(tile_rows(i), i, zero_rs, zero_accs, causal))

    @pl.when(i > 0)
    def _():
        lss_a, lk2_a = scores(tile_rows(i), causal)
        lss_b, lk2_b = scores(tile_rows(i - 1), None)
        css_a = cumsums(lk2_a)
        css_b = cumsums(lk2_b)
        a2_a = weights(lss_a, css_a, zero_rs, causal)
        rs = [cs[:, SB_BLOCK:] for cs in css_a]
        a2_b = weights(lss_b, css_b, rs, None)
        rs = [r + cs[:, SB_BLOCK:] for r, cs in zip(rs, css_b)]
        accs = [jnp.dot(a, v2_ref[p, i], preferred_element_type=F32)
                + jnp.dot(b, v2_ref[p, i - 1], preferred_element_type=F32)
                for p, (a, b) in enumerate(zip(a2_a, a2_b))]
        store_state(rs, accs)

    def cond(carry):
        t, alive = carry
        return jnp.logical_and(t < i, alive > 0)

    def body(carry):
        t, _ = carry
        j = i - 1 - t
        rs, accs = one_tile(tile_rows(j), j, *load_state(), None)
        store_state(rs, accs)
        return t + 1, alive_flag(rs)
    lax.while_loop(cond, body, (jnp.int32(1), alive_flag([r_ref[p] for p in pairs])))

    o_ref[...] = acc_ref[...].astype(o_ref.dtype)


def _stick_breaking(proj, qg2, kg2, cum_mat, half_ones2, batch, seq):
    nq = seq // SB_BLOCK
    npair = SB_HEADS // 2
    qcol = (A_QKV + DN_WIDTH) // SB_WIDTH
    return pl.pallas_call(
        _sb_kernel,
        grid=(batch, nq),
        in_specs=[
            pl.BlockSpec((SB_BLOCK, SB_WIDTH), lambda b, i: (b * nq + i, qcol)),
            pl.BlockSpec((seq, SB_WIDTH), lambda b, i: (b, qcol + 1)),
            pl.BlockSpec((seq, SB_WIDTH), lambda b, i: (b, qcol + 2)),
            pl.BlockSpec((1, LANES), lambda b, i: (0, 0)),
            pl.BlockSpec((1, LANES), lambda b, i: (0, 0)),
            pl.BlockSpec((2 * SB_BLOCK, 2 * SB_BLOCK), lambda b, i: (0, 0)),
            pl.BlockSpec((2 * LANES, LANES), lambda b, i: (0, 0)),
        ],
        out_specs=pl.BlockSpec((SB_BLOCK, SB_WIDTH), lambda b, i: (b * nq + i, 0)),
        out_shape=jax.ShapeDtypeStruct((batch * seq, SB_WIDTH), BF16),
        scratch_shapes=[
            pltpu.VMEM((seq, SB_WIDTH), BF16),
            pltpu.VMEM((npair, nq, 2 * SB_BLOCK, LANES), BF16),
            pltpu.VMEM((npair, 2 * SB_BLOCK, SB_BLOCK), F32),
            pltpu.VMEM((SB_BLOCK, SB_WIDTH), F32),
        ],
        compiler_params=pltpu.CompilerParams(
            dimension_semantics=("arbitrary", "arbitrary"), vmem_limit_bytes=VMEM_LIMIT),
        name="stickbreak",
    )(proj, proj, proj, qg2, kg2, cum_mat, half_ones2)


def _merge_kernel(x_ref, oa_ref, ob_ref, zb_ref, g_ref, bg_ref, wa_ref, wb_ref, wo_ref, out_ref):
    zb = zb_ref[...].astype(F32)
    ob = (ob_ref[...].astype(F32) * _silu(zb)).astype(BF16)
    ya = jnp.dot(oa_ref[...], wa_ref[...], preferred_element_type=F32)
    yb = jnp.dot(ob, wb_ref[...], preferred_element_type=F32)
    gates = jax.nn.sigmoid(g_ref[...].astype(F32) + bg_ref[...])
    merged = gates[:, :D_MODEL] * ya + gates[:, D_MODEL:] * yb
    out_ref[...] = x_ref[...] + jnp.dot(merged.astype(BF16), wo_ref[...], preferred_element_type=F32)


def _merge(x2, o_a, o_b, proj, b_gate, w_up_a, w_up_b, w_out):
    t = x2.shape[0]
    zb_col = (A_QKV + DN_WIDTH + 3 * SB_WIDTH) // SB_WIDTH
    g_col = (A_QKV + DN_WIDTH + 4 * SB_WIDTH) // (2 * D_MODEL)
    const = lambda i: (0, 0)
    return pl.pallas_call(
        _merge_kernel,
        grid=(t // OUT_TM,),
        in_specs=[
            pl.BlockSpec((OUT_TM, D_MODEL), lambda i: (i, 0)),
            pl.BlockSpec((OUT_TM, DN_WIDTH), lambda i: (i, 0)),
            pl.BlockSpec((OUT_TM, SB_WIDTH), lambda i: (i, 0)),
            pl.BlockSpec((OUT_TM, SB_WIDTH), lambda i: (i, zb_col)),
            pl.BlockSpec((OUT_TM, 2 * D_MODEL), lambda i: (i, g_col)),
            pl.BlockSpec((1, 2 * D_MODEL), const),
            pl.BlockSpec((DN_WIDTH, D_MODEL), const),
            pl.BlockSpec((SB_WIDTH, D_MODEL), const),
            pl.BlockSpec((D_MODEL, D_MODEL), const),
        ],
        out_specs=pl.BlockSpec((OUT_TM, D_MODEL), lambda i: (i, 0)),
        out_shape=jax.ShapeDtypeStruct((t, D_MODEL), F32),
        compiler_params=pltpu.CompilerParams(
            dimension_semantics=("arbitrary",), vmem_limit_bytes=VMEM_LIMIT),
        name="merge",
    )(x2, o_a, o_b, proj, proj, b_gate, w_up_a, w_up_b, w_out)


def _layer(x, norm_gain, w_in, b_gate, conv_w, a_log, dt_bias, dn_out_gain,
           sb_q_gain, sb_k_gain, w_up_a, w_up_b, w_out):
    batch, seq, _ = x.shape
    x2 = x.reshape(batch * seq, D_MODEL)

    w_main = jnp.concatenate([w_in[:, :BA_COL0], w_in[:, BA_COL0 + BA_COLS:]], axis=1).astype(BF16)
    w_ba = jnp.pad(w_in[:, BA_COL0:BA_COL0 + BA_COLS], ((0, 0), (0, LANES - BA_COLS)))
    wba_hi = w_ba.astype(BF16)
    wba_lo = (w_ba - wba_hi.astype(F32)).astype(BF16)
    par = jnp.zeros((8, LANES), F32)
    par = par.at[0, DN_HEADS:2 * DN_HEADS].set(a_log).at[1, DN_HEADS:2 * DN_HEADS].set(dt_bias)
    qg2 = jnp.concatenate([sb_q_gain, sb_q_gain])[None, :]
    kg2 = jnp.concatenate([sb_k_gain, sb_k_gain])[None, :]
    kk = jnp.arange(SB_BLOCK)
    cum1 = jnp.concatenate([(kk[:, None] > kk[None, :]).astype(BF16), jnp.ones((SB_BLOCK, SB_BLOCK), BF16)], axis=1)
    cum_mat = jnp.concatenate([cum1, cum1], axis=0)
    ln = jnp.arange(LANES)
    half1 = ((ln[:, None] // SB_DH) == (ln[None, :] // SB_DH)).astype(BF16)
    half_ones2 = jnp.concatenate([half1, half1], axis=0)
    tt = jnp.arange(PAIR)
    ltri1 = ((tt[:, None] // CHUNK == tt[None, :] // CHUNK) & (tt[:, None] >= tt[None, :])).astype(BF16)
    ltri3 = jnp.concatenate([ltri1, ltri1, ltri1], axis=1)

    proj, ba = _inproj(x2, norm_gain[None, :], w_main, wba_hi, wba_lo)
    o_a = _deltanet(proj, ba, conv_w, par, dn_out_gain[None, :], ltri3, batch, seq)
    o_b = _stick_breaking(proj, qg2, kg2, cum_mat, half_ones2, batch, seq)
    out = _merge(x2, o_a, o_b, proj, b_gate[None, :], w_up_a.astype(BF16), w_up_b.astype(BF16),
                 w_out.astype(BF16))
    return out.reshape(batch, seq, D_MODEL)


def kernel(x, norm_gain, w_in, b_gate, conv_w, a_log, dt_bias, dn_out_gain, sb_q_gain, sb_k_gain,
           w_up_a, w_up_b, w_out):
    h = x
    for layer in range(norm_gain.shape[0]):
        h = _layer(h, norm_gain[layer], w_in[layer], b_gate[layer], conv_w[layer], a_log[layer],
                   dt_bias[layer], dn_out_gain[layer], sb_q_gain[layer], sb_k_gain[layer],
                   w_up_a[layer], w_up_b[layer], w_out[layer])
    return h
```

```python
import functools

import jax
import jax.numpy as jnp
from jax import lax
from jax.experimental import pallas as pl
from jax.experimental.pallas import tpu as pltpu

F32 = jnp.float32
BF16 = jnp.bfloat16

D_MODEL = 1024
CHUNK = 64
SB_BLOCK = 128
EPS = 1e-6
LOG2E = 1.4426950408889634
ZERO_WEIGHT_LOG2 = -151.0
SB_LEAD = 3
DN_HEADS = 4
DN_DK = 128
DN_DV = 128
DN_CONV = 4
SB_HEADS = 8
SB_DH = 64
DN_WIDTH = DN_HEADS * DN_DV
SB_WIDTH = SB_HEADS * SB_DH
A_QKV = 2 * DN_HEADS * DN_DK + DN_WIDTH
BA_COL0 = A_QKV + DN_WIDTH
BA_COLS = 2 * DN_HEADS
MAIN_WIDTH = A_QKV + DN_WIDTH + 3 * SB_WIDTH + SB_WIDTH + 2 * D_MODEL

LANES = 128
PAIR = 2 * CHUNK

IN_TM, IN_TN = 1024, 1024
DN_TS = 1024
DN_WAVE = 256
OUT_TM = 512

VMEM_LIMIT = 48 * 1024 * 1024


def _split3(x):
    hi = x.astype(BF16)
    r1 = x - hi.astype(F32)
    mid = r1.astype(BF16)
    lo = (r1 - mid.astype(F32)).astype(BF16)
    return hi, mid, lo


def _split2(x):
    hi = x.astype(BF16)
    lo = (x - hi.astype(F32)).astype(BF16)
    return hi, lo


def _silu(x):
    h = 0.5 * x
    return h + h * jnp.tanh(h)


def _inproj_kernel(x_ref, gain_ref, w_ref, wba_hi_ref, wba_lo_ref, proj_ref, ba_ref, xn_ref):
    @pl.when(pl.program_id(1) == 0)
    def _():
        x = x_ref[...]
        xn = x * lax.rsqrt(jnp.mean(x * x, axis=-1, keepdims=True) + EPS) * gain_ref[...]
        hi, lo = _split2(xn)
        xn_ref[...] = hi
        ba_ref[...] = (jnp.dot(hi, wba_hi_ref[...], preferred_element_type=F32)
                       + jnp.dot(lo, wba_hi_ref[...], preferred_element_type=F32)
                       + jnp.dot(hi, wba_lo_ref[...], preferred_element_type=F32))

    proj_ref[...] = jnp.dot(xn_ref[...], w_ref[...], preferred_element_type=F32).astype(proj_ref.dtype)


def _inproj(x2, gain, w_main, wba_hi, wba_lo):
    t = x2.shape[0]
    return pl.pallas_call(
        _inproj_kernel,
        grid=(t // IN_TM, MAIN_WIDTH // IN_TN),
        in_specs=[
            pl.BlockSpec((IN_TM, D_MODEL), lambda i, j: (i, 0)),
            pl.BlockSpec((1, D_MODEL), lambda i, j: (0, 0)),
            pl.BlockSpec((D_MODEL, IN_TN), lambda i, j: (0, j)),
            pl.BlockSpec((D_MODEL, LANES), lambda i, j: (0, 0)),
            pl.BlockSpec((D_MODEL, LANES), lambda i, j: (0, 0)),
        ],
        out_specs=[
            pl.BlockSpec((IN_TM, IN_TN), lambda i, j: (i, j)),
            pl.BlockSpec((IN_TM, LANES), lambda i, j: (i, 0)),
        ],
        out_shape=[
            jax.ShapeDtypeStruct((t, MAIN_WIDTH), BF16),
            jax.ShapeDtypeStruct((t, LANES), F32),
        ],
        scratch_shapes=[pltpu.VMEM((IN_TM, D_MODEL), BF16)],
        compiler_params=pltpu.CompilerParams(
            dimension_semantics=("arbitrary", "arbitrary"), vmem_limit_bytes=VMEM_LIMIT),
        name="inproj",
    )(x2, gain, w_main, wba_hi, wba_lo)


def _tri_inverse(ms, row, col):
    blk8 = (row >> 3) == (col >> 3)
    eye = jnp.where(row == col, 1.0, 0.0)
    pds = [jnp.where(blk8, -m, 0.0) for m in ms]
    pdb = [pd.astype(BF16) for pd in pds]
    xs = [eye + pd for pd in pds]
    p2b = [jnp.dot(b, b, preferred_element_type=F32).astype(BF16) for b in pdb]
    yield
    xs = [x + jnp.dot(x.astype(BF16), p2, preferred_element_type=F32) for x, p2 in zip(xs, p2b)]
    yield
    p4b = [jnp.dot(b, b, preferred_element_type=F32).astype(BF16) for b in p2b]
    yield
    xs = [x + jnp.dot(x.astype(BF16), p4, preferred_element_type=F32) for x, p4 in zip(xs, p4b)]
    yield
    for s in (8, 16, 32):
        sh = s.bit_length()
        off = ((row >> sh) == (col >> sh)) & ((row & (2 * s - 1)) >= s) & ((col & (2 * s - 1)) < s)
        xb = [x.astype(BF16) for x in xs]
        ts_ = [jnp.dot(b, jnp.where(off, m, 0.0).astype(BF16), preferred_element_type=F32).astype(BF16)
               for b, m in zip(xb, ms)]
        yield
        xs = [x - jnp.dot(t, b, preferred_element_type=F32) for x, t, b in zip(xs, ts_, xb)]
        yield
    return xs


def _round_robin(*gens):
    live = list(gens)
    while live:
        for g in list(live):
            try:
                next(g)
            except StopIteration:
                live.remove(g)


def _deltanet_kernel(qkv_ref, z_ref, ba_ref, convw_ref, par_ref, gain_ref, ltri_ref, o_ref,
                     buf_ref, act_ref, state_ref, ol_ref, qt_ref, mc_ref, bc_ref, ge_ref):
    ts = qkv_ref.shape[0]
    nqk = DN_HEADS * DN_DK
    heads = range(DN_HEADS)
    nslab = A_QKV // LANES
    nwave = ts // DN_WAVE

    @pl.when(pl.program_id(1) == 0)
    def _():
        buf_ref[:, 0:8, :] = jnp.zeros((nslab, 8, LANES), F32)
        state_ref[...] = jnp.zeros_like(state_ref)

    for s in range(nslab):
        buf_ref[s, 8:8 + ts, :] = qkv_ref[:, s * LANES:(s + 1) * LANES].astype(F32)

    def activations(w):
        for r0 in range(w * DN_WAVE, (w + 1) * DN_WAVE, PAIR):
            for s in range(nslab):
                cs = slice(s * LANES, (s + 1) * LANES)
                acc = None
                for i in range(DN_CONV):
                    lo = 8 - (DN_CONV - 1) + i + r0
                    term = buf_ref[s, lo:lo + PAIR, :] * convw_ref[i:i + 1, cs]
                    acc = term if acc is None else acc + term
                y = _silu(acc)
                if s < 2 * DN_HEADS:
                    y = y * lax.rsqrt(jnp.sum(y * y, axis=-1, keepdims=True) + EPS)
                if s < DN_HEADS:
                    y = y * (DN_DK ** -0.5)
                act_ref[r0:r0 + PAIR, cs] = y
                yield

    ba = ba_ref[...]
    beta_all = jax.nn.sigmoid(ba)
    xs = ba + par_ref[1:2, :]
    softplus = jnp.maximum(xs, 0.0) + jnp.log(1.0 + jnp.exp(-jnp.abs(xs)))
    g_all = -jnp.exp(par_ref[0:1, :]) * softplus
    gc_of = [jnp.dot(ltri_ref[...], jnp.concatenate(_split3(g_all[p * PAIR:(p + 1) * PAIR, :]), axis=0),
                     preferred_element_type=F32) for p in range(ts // PAIR)]

    row = lax.broadcasted_iota(jnp.int32, (PAIR, PAIR), 0)
    col = lax.broadcasted_iota(jnp.int32, (PAIR, PAIR), 1)
    same = (row >> 6) == (col >> 6)
    lower = same & (row >= col)
    strict = same & (row > col)
    rcol = lax.broadcasted_iota(jnp.int32, (PAIR, 1), 0)

    def chunk_local(w):
        pairs = range(w * (DN_WAVE // PAIR), (w + 1) * (DN_WAVE // PAIR))
        units = [(p, h) for p in pairs for h in heads]
        rows_of = [slice(p * PAIR, (p + 1) * PAIR) for p, _ in units]
        gc_t = {p: gc_of[p].T for p in pairs}
        q = [act_ref[rs, h * DN_DK:(h + 1) * DN_DK] for rs, (_, h) in zip(rows_of, units)]
        k = [act_ref[rs, nqk + h * DN_DK:nqk + (h + 1) * DN_DK] for rs, (_, h) in zip(rows_of, units)]
        v = [act_ref[rs, 2 * nqk + h * DN_DV:2 * nqk + (h + 1) * DN_DV] for rs, (_, h) in zip(rows_of, units)]
        beta = [beta_all[rs, h:h + 1] for rs, (_, h) in zip(rows_of, units)]
        gcc = [gc_of[p][:, DN_HEADS + h:DN_HEADS + h + 1] for p, h in units]
        decay = [jnp.where(lower, jnp.exp(g - gc_t[p][DN_HEADS + h:DN_HEADS + h + 1, :]), 0.0)
                 for g, (p, h) in zip(gcc, units)]
        eg = [jnp.exp(g) for g in gcc]
        gend = [jnp.where(rcol < CHUNK, g[CHUNK - 1:CHUNK, :], g[PAIR - 1:PAIR, :]) for g in gcc]
        kb = [x.astype(BF16) for x in k]
        yield
        qkk = [lax.dot_general(jnp.concatenate([qq.astype(BF16), kk], axis=0), kk,
                               (((1,), (1,)), ((), ())), preferred_element_type=F32) for qq, kk in zip(q, kb)]
        yield
        ms = [jnp.where(strict, b * x[PAIR:] * d, 0.0) for b, x, d in zip(beta, qkk, decay)]
        tinv = yield from _tri_inverse(ms, row, col)
        sol = [jnp.dot(t.astype(BF16),
                       jnp.concatenate([(b * vv).astype(BF16), (b * e * kk).astype(BF16)], axis=1),
                       preferred_element_type=F32) for t, b, vv, e, kk in zip(tinv, beta, v, eg, k)]
        yield
        solb = [x.astype(BF16) for x in sol]
        qkb = [jnp.where(lower, x[:PAIR] * d, 0.0).astype(BF16) for x, d in zip(qkk, decay)]
        qwu = [jnp.dot(a, b, preferred_element_type=F32) for a, b in zip(qkb, solb)]
        yield
        kd = [kk * jnp.exp(ge - g) for kk, ge, g in zip(k, gend, gcc)]
        for c in range(PAIR // CHUNK):
            kdc = [jnp.where((row >> 6) == c, x, 0.0).astype(BF16) for x in kd]
            kwu = [lax.dot_general(a, b, (((0,), (0,)), ((), ())), preferred_element_type=F32)
                   for a, b in zip(kdc, solb)]
            yield
            for (p, h), x, g in zip(units, kwu, gcc):
                cc = p * (PAIR // CHUNK) + c
                bc_ref[h, cc] = x[:, :DN_DV]
                mc_ref[h, cc] = x[:, DN_DV:].astype(BF16)
                ge_ref[h, cc] = jnp.broadcast_to(jnp.exp(g[(c + 1) * CHUNK - 1:(c + 1) * CHUNK, :]), (8, LANES))
        for (p, h), rs, x, qq, e in zip(units, rows_of, qwu, q, eg):
            ol_ref[h, rs, :] = x[:, :DN_DV]
            qt_ref[h, rs, :] = (qq * e - x[:, DN_DV:]).astype(BF16)
        yield

    gain = gain_ref[...]
    st = [state_ref[h] for h in heads]

    def recurrence(w):
        for cc in range(w * (DN_WAVE // CHUNK), (w + 1) * (DN_WAVE // CHUNK)):
            cr = slice(cc * CHUNK, (cc + 1) * CHUNK)
            sb = [x.astype(BF16) for x in st]
            o = [ol_ref[h, cr, :] + jnp.dot(qt_ref[h, cr, :], sb[h], preferred_element_type=F32) for h in heads]
            new = [st[h] * ge_ref[h, cc, 0:1, :] + bc_ref[h, cc]
                   - jnp.dot(mc_ref[h, cc], sb[h], preferred_element_type=F32) for h in heads]
            for h in heads:
                st[h] = new[h]
            yield
            for h in heads:
                hs = slice(h * DN_DV, (h + 1) * DN_DV)
                on = o[h] * lax.rsqrt(jnp.mean(o[h] * o[h], axis=-1, keepdims=True) + EPS) * gain
                zz = z_ref[cr, hs].astype(F32)
                o_ref[cr, hs] = (on * _silu(zz)).astype(o_ref.dtype)
            yield

    _round_robin(activations(0))
    for w in range(nwave):
        others = []
        if w + 1 < nwave:
            others.append(activations(w + 1))
        if w > 0:
            others.append(recurrence(w - 1))
        _round_robin(chunk_local(w), *others)
    _round_robin(recurrence(nwave - 1))
    for h in heads:
        state_ref[h] = st[h]
    for s in range(nslab):
        buf_ref[s, 0:8, :] = buf_ref[s, ts:ts + 8, :]


def _deltanet(proj, ba, conv_w, par, gain, ltri3, batch, seq):
    nsb = seq // DN_TS
    nchunk = DN_TS // CHUNK
    return pl.pallas_call(
        _deltanet_kernel,
        grid=(batch, nsb),
        in_specs=[
            pl.BlockSpec((DN_TS, A_QKV), lambda b, s: (b * nsb + s, 0)),
            pl.BlockSpec((DN_TS, DN_WIDTH), lambda b, s: (b * nsb + s, A_QKV // DN_WIDTH)),
            pl.BlockSpec((DN_TS, LANES), lambda b, s: (b * nsb + s, 0)),
            pl.BlockSpec((DN_CONV, A_QKV), lambda b, s: (0, 0)),
            pl.BlockSpec((8, LANES), lambda b, s: (0, 0)),
            pl.BlockSpec((1, DN_DV), lambda b, s: (0, 0)),
            pl.BlockSpec((PAIR, 3 * PAIR), lambda b, s: (0, 0)),
        ],
        out_specs=pl.BlockSpec((DN_TS, DN_WIDTH), lambda b, s: (b * nsb + s, 0)),
        out_shape=jax.ShapeDtypeStruct((batch * seq, DN_WIDTH), BF16),
        scratch_shapes=[
            pltpu.VMEM((A_QKV // LANES, DN_TS + 8, LANES), F32),
            pltpu.VMEM((DN_TS, A_QKV), F32),
            pltpu.VMEM((DN_HEADS, DN_DK, DN_DV), F32),
            pltpu.VMEM((DN_HEADS, DN_TS, DN_DV), F32),
            pltpu.VMEM((DN_HEADS, DN_TS, DN_DK), BF16),
            pltpu.VMEM((DN_HEADS, nchunk, DN_DK, DN_DK), BF16),
            pltpu.VMEM((DN_HEADS, nchunk, DN_DK, DN_DV), F32),
            pltpu.VMEM((DN_HEADS, nchunk, 8, LANES), F32),
        ],
        compiler_params=pltpu.CompilerParams(
            dimension_semantics=("arbitrary", "arbitrary"), vmem_limit_bytes=VMEM_LIMIT),
        name="deltanet",
    )(proj, proj, ba, conv_w, par, gain, ltri3)


def _rms_halves(x, half_ones2):
    hi, lo = _split2(x * x)
    ss = jnp.dot(jnp.concatenate([hi, lo], axis=1), half_ones2, preferred_element_type=F32)
    return x * lax.rsqrt(ss * (1.0 / SB_DH) + EPS)


def _sb_kernel(q_ref, k_ref, v_ref, qg_ref, kg_ref, cum_ref, half_ref, o_ref,
               q2_ref, kn_ref, v2_ref, r_ref, acc_ref, flag_ref):
    i = pl.program_id(1)
    nblk = k_ref.shape[0] // SB_BLOCK
    npair = k_ref.shape[1] // LANES
    pairs = range(npair)
    lane = lax.broadcasted_iota(jnp.int32, (SB_BLOCK, LANES), 1)
    first = lane < SB_DH

    def cols(p):
        return slice(p * LANES, (p + 1) * LANES)

    @pl.when(i == 0)
    def _():
        qscale = qg_ref[...] * (SB_DH ** -0.5 * LOG2E)

        def prep(j, carry):
            rows = pl.ds(pl.multiple_of(j * SB_BLOCK, SB_BLOCK), SB_BLOCK)
            for p in pairs:
                qn = (_rms_halves(q_ref[rows, cols(p)].astype(F32), half_ref[...]) * qscale).astype(BF16)
                zero = jnp.zeros_like(qn)
                q2_ref[p, j, 0:SB_BLOCK, :] = jnp.where(first, qn, zero)
                q2_ref[p, j, SB_BLOCK:2 * SB_BLOCK, :] = jnp.where(first, zero, qn)
                kn = _rms_halves(k_ref[rows, cols(p)].astype(F32), half_ref[...]) * kg_ref[...]
                kn_ref[rows, cols(p)] = kn.astype(BF16)
                vb = v_ref[rows, cols(p)]
                v2_ref[p, j, 0:SB_BLOCK, :] = jnp.where(first, vb, zero)
                v2_ref[p, j, SB_BLOCK:2 * SB_BLOCK, :] = jnp.where(first, zero, vb)
            return carry
        lax.fori_loop(0, nblk, prep, 0)

    q2 = [q2_ref[p, i] for p in pairs]

    def scores(rows, mask):
        zs = [lax.dot_general(q2[p], kn_ref[rows, cols(p)], (((1,), (1,)), ((), ())),
                              preferred_element_type=F32) for p in pairs]
        lss = [jnp.minimum(z, 0.0) - jnp.log(1.0 + jnp.exp2(-jnp.abs(z))) * LOG2E for z in zs]
        lks = [ls - z for ls, z in zip(lss, zs)]
        if mask is not None:
            lks = [jnp.where(mask, lk, 0.0) for lk in lks]
        return lss, [jnp.concatenate(_split2(lk), axis=1) for lk in lks]

    def cumsums(lk2s):
        return [jnp.dot(lk2, cum_ref[...], preferred_element_type=F32) for lk2 in lk2s]

    def weights(lss, css, rs, mask):
        avs = [jnp.exp2(ls + cs[:, :SB_BLOCK] + r) for ls, cs, r in zip(lss, css, rs)]
        if mask is not None:
            avs = [jnp.where(mask, a, 0.0) for a in avs]
        out = []
        for a in avs:
            ab = a.astype(BF16)
            out.append(jnp.concatenate([ab[:SB_BLOCK], ab[SB_BLOCK:]], axis=1))
        return out

    def tile_rows(j):
        return pl.ds(pl.multiple_of(j * SB_BLOCK, SB_BLOCK), SB_BLOCK)

    def sweep(tiles, masks, rs, accs):
        sc = [scores(tile_rows(j), m) for j, m in zip(tiles, masks)]
        css = [cumsums(lk2s) for _, lk2s in sc]
        for j, m, (lss, _), cs in zip(tiles, masks, sc, css):
            a2s = weights(lss, cs, rs, m)
            rs = [r + c[:, SB_BLOCK:] for r, c in zip(rs, cs)]
            accs = [acc + jnp.dot(a2, v2_ref[p, j], preferred_element_type=F32)
                    for p, (acc, a2) in enumerate(zip(accs, a2s))]
        return rs, accs

    def load_state():
        return [r_ref[p] for p in pairs], [acc_ref[:, cols(p)] for p in pairs]

    def store_state(rs, accs):
        for p in pairs:
            r_ref[p] = rs[p]
            acc_ref[:, cols(p)] = accs[p]

    qrow = lax.broadcasted_iota(jnp.int32, (2 * SB_BLOCK, SB_BLOCK), 0) & (SB_BLOCK - 1)
    kcol = lax.broadcasted_iota(jnp.int32, (2 * SB_BLOCK, SB_BLOCK), 1)
    causal = kcol < qrow
    zero_rs = [jnp.zeros((2 * SB_BLOCK, SB_BLOCK), F32)] * npair
    zero_accs = [jnp.zeros((SB_BLOCK, LANES), F32)] * npair

    def alive_flag(rs):
        return (jnp.max(functools.reduce(jnp.maximum, rs)) >= ZERO_WEIGHT_LOG2).astype(jnp.int32)

    for n in range(1, SB_LEAD + 1):
        @pl.when((i == n - 1) if n < SB_LEAD else (i >= n - 1))
        def _(n=n):
            rs, accs = sweep([i - d for d in range(n)], [causal] + [None] * (n - 1), zero_rs, zero_accs)
            store_state(rs, accs)
            flag_ref[0] = alive_flag(rs)

    def cond(carry):
        t, alive = carry
        return jnp.logical_and(t < i, alive > 0)

    def body(carry):
        t, _ = carry
        j = i - 1 - t
        rs, accs = sweep([j], [None], *load_state())
        store_state(rs, accs)
        return t + 1, alive_flag(rs)
    lax.while_loop(cond, body, (jnp.int32(SB_LEAD - 1), flag_ref[0]))

    o_ref[...] = acc_ref[...].astype(o_ref.dtype)


def _stick_breaking(proj, qg2, kg2, cum_mat, half_ones2, batch, seq):
    nq = seq // SB_BLOCK
    npair = SB_HEADS // 2
    qcol = (A_QKV + DN_WIDTH) // SB_WIDTH
    return pl.pallas_call(
        _sb_kernel,
        grid=(batch, nq),
        in_specs=[
            pl.BlockSpec((seq, SB_WIDTH), lambda b, i: (b, qcol)),
            pl.BlockSpec((seq, SB_WIDTH), lambda b, i: (b, qcol + 1)),
            pl.BlockSpec((seq, SB_WIDTH), lambda b, i: (b, qcol + 2)),
            pl.BlockSpec((1, LANES), lambda b, i: (0, 0)),
            pl.BlockSpec((1, LANES), lambda b, i: (0, 0)),
            pl.BlockSpec((2 * SB_BLOCK, 2 * SB_BLOCK), lambda b, i: (0, 0)),
            pl.BlockSpec((2 * LANES, LANES), lambda b, i: (0, 0)),
        ],
        out_specs=pl.BlockSpec((SB_BLOCK, SB_WIDTH), lambda b, i: (b * nq + i, 0)),
        out_shape=jax.ShapeDtypeStruct((batch * seq, SB_WIDTH), BF16),
        scratch_shapes=[
            pltpu.VMEM((npair, nq, 2 * SB_BLOCK, LANES), BF16),
            pltpu.VMEM((seq, SB_WIDTH), BF16),
            pltpu.VMEM((npair, nq, 2 * SB_BLOCK, LANES), BF16),
            pltpu.VMEM((npair, 2 * SB_BLOCK, SB_BLOCK), F32),
            pltpu.VMEM((SB_BLOCK, SB_WIDTH), F32),
            pltpu.SMEM((1,), jnp.int32),
        ],
        compiler_params=pltpu.CompilerParams(
            dimension_semantics=("arbitrary", "arbitrary"), vmem_limit_bytes=VMEM_LIMIT),
        name="stickbreak",
    )(proj, proj, proj, qg2, kg2, cum_mat, half_ones2)


def _merge_kernel(x_ref, oa_ref, ob_ref, zb_ref, g_ref, bg_ref, wa_ref, wb_ref, wo_ref, out_ref):
    zb = zb_ref[...].astype(F32)
    ob = (ob_ref[...].astype(F32) * _silu(zb)).astype(BF16)
    ya = jnp.dot(oa_ref[...], wa_ref[...], preferred_element_type=F32)
    yb = jnp.dot(ob, wb_ref[...], preferred_element_type=F32)
    gates = jax.nn.sigmoid(g_ref[...].astype(F32) + bg_ref[...])
    merged = gates[:, :D_MODEL] * ya + gates[:, D_MODEL:] * yb
    out_ref[...] = x_ref[...] + jnp.dot(merged.astype(BF16), wo_ref[...], preferred_element_type=F32)


def _merge(x2, o_a, o_b, proj, b_gate, w_up_a, w_up_b, w_out):
    t = x2.shape[0]
    zb_col = (A_QKV + DN_WIDTH + 3 * SB_WIDTH) // SB_WIDTH
    g_col = (A_QKV + DN_WIDTH + 4 * SB_WIDTH) // (2 * D_MODEL)
    const = lambda i: (0, 0)
    return pl.pallas_call(
        _merge_kernel,
        grid=(t // OUT_TM,),
        in_specs=[
            pl.BlockSpec((OUT_TM, D_MODEL), lambda i: (i, 0)),
            pl.BlockSpec((OUT_TM, DN_WIDTH), lambda i: (i, 0)),
            pl.BlockSpec((OUT_TM, SB_WIDTH), lambda i: (i, 0)),
            pl.BlockSpec((OUT_TM, SB_WIDTH), lambda i: (i, zb_col)),
            pl.BlockSpec((OUT_TM, 2 * D_MODEL), lambda i: (i, g_col)),
            pl.BlockSpec((1, 2 * D_MODEL), const),
            pl.BlockSpec((DN_WIDTH, D_MODEL), const),
            pl.BlockSpec((SB_WIDTH, D_MODEL), const),
            pl.BlockSpec((D_MODEL, D_MODEL), const),
        ],
        out_specs=pl.BlockSpec((OUT_TM, D_MODEL), lambda i: (i, 0)),
        out_shape=jax.ShapeDtypeStruct((t, D_MODEL), F32),
        compiler_params=pltpu.CompilerParams(
            dimension_semantics=("arbitrary",), vmem_limit_bytes=VMEM_LIMIT),
        name="merge",
    )(x2, o_a, o_b, proj, proj, b_gate, w_up_a, w_up_b, w_out)


def _layer(x, norm_gain, w_in, b_gate, conv_w, a_log, dt_bias, dn_out_gain,
           sb_q_gain, sb_k_gain, w_up_a, w_up_b, w_out):
    batch, seq, _ = x.shape
    x2 = x.reshape(batch * seq, D_MODEL)

    w_main = jnp.concatenate([w_in[:, :BA_COL0], w_in[:, BA_COL0 + BA_COLS:]], axis=1).astype(BF16)
    w_ba = jnp.pad(w_in[:, BA_COL0:BA_COL0 + BA_COLS], ((0, 0), (0, LANES - BA_COLS)))
    wba_hi = w_ba.astype(BF16)
    wba_lo = (w_ba - wba_hi.astype(F32)).astype(BF16)
    par = jnp.zeros((8, LANES), F32)
    par = par.at[0, DN_HEADS:2 * DN_HEADS].set(a_log).at[1, DN_HEADS:2 * DN_HEADS].set(dt_bias)
    qg2 = jnp.concatenate([sb_q_gain, sb_q_gain])[None, :]
    kg2 = jnp.concatenate([sb_k_gain, sb_k_gain])[None, :]
    kk = jnp.arange(SB_BLOCK)
    cum1 = jnp.concatenate([(kk[:, None] > kk[None, :]).astype(BF16), jnp.ones((SB_BLOCK, SB_BLOCK), BF16)], axis=1)
    cum_mat = jnp.concatenate([cum1, cum1], axis=0)
    ln = jnp.arange(LANES)
    half1 = ((ln[:, None] // SB_DH) == (ln[None, :] // SB_DH)).astype(BF16)
    half_ones2 = jnp.concatenate([half1, half1], axis=0)
    tt = jnp.arange(PAIR)
    ltri1 = ((tt[:, None] // CHUNK == tt[None, :] // CHUNK) & (tt[:, None] >= tt[None, :])).astype(BF16)
    ltri3 = jnp.concatenate([ltri1, ltri1, ltri1], axis=1)

    proj, ba = _inproj(x2, norm_gain[None, :], w_main, wba_hi, wba_lo)
    o_a = _deltanet(proj, ba, conv_w, par, dn_out_gain[None, :], ltri3, batch, seq)
    o_b = _stick_breaking(proj, qg2, kg2, cum_mat, half_ones2, batch, seq)
    out = _merge(x2, o_a, o_b, proj, b_gate[None, :], w_up_a.astype(BF16), w_up_b.astype(BF16),
                 w_out.astype(BF16))
    return out.reshape(batch, seq, D_MODEL)


def kernel(x, norm_gain, w_in, b_gate, conv_w, a_log, dt_bias, dn_out_gain, sb_q_gain, sb_k_gain,
           w_up_a, w_up_b, w_out):
    h = x
    for layer in range(norm_gain.shape[0]):
        h = _layer(h, norm_gain[layer], w_in[layer], b_gate[layer], conv_w[layer], a_log[layer],
                   dt_bias[layer], dn_out_gain[layer], sb_q_gain[layer], sb_k_gain[layer],
                   w_up_a[layer], w_up_b[layer], w_out[layer])
    return h
```

```python
import functools

import jax
import jax.numpy as jnp
from jax import lax
from jax.experimental import pallas as pl
from jax.experimental.pallas import tpu as pltpu

F32 = jnp.float32
BF16 = jnp.bfloat16

D_MODEL = 1024
CHUNK = 64
SB_BLOCK = 128
EPS = 1e-6
LOG2E = 1.4426950408889634
ZERO_WEIGHT_LOG2 = -151.0
SB_LEAD = 3
DN_HEADS = 4
DN_DK = 128
DN_DV = 128
DN_CONV = 4
SB_HEADS = 8
SB_DH = 64
DN_WIDTH = DN_HEADS * DN_DV
SB_WIDTH = SB_HEADS * SB_DH
A_QKV = 2 * DN_HEADS * DN_DK + DN_WIDTH
BA_COL0 = A_QKV + DN_WIDTH
BA_COLS = 2 * DN_HEADS
MAIN_WIDTH = A_QKV + DN_WIDTH + 3 * SB_WIDTH + SB_WIDTH
GATE_COL0 = BA_COL0 + BA_COLS + 4 * SB_WIDTH

LANES = 128
PAIR = 2 * CHUNK

IN_TM, IN_TN = 1024, 1024
DN_TS = 1024
DN_WAVE = 256
OUT_TM = 512

VMEM_LIMIT = 48 * 1024 * 1024


def _split3(x):
    hi = x.astype(BF16)
    r1 = x - hi.astype(F32)
    mid = r1.astype(BF16)
    lo = (r1 - mid.astype(F32)).astype(BF16)
    return hi, mid, lo


def _split2(x):
    hi = x.astype(BF16)
    lo = (x - hi.astype(F32)).astype(BF16)
    return hi, lo


def _silu(x):
    h = 0.5 * x
    return h + h * jnp.tanh(h)


def _inproj_kernel(x_ref, gain_ref, w_ref, wba_hi_ref, wba_lo_ref, proj_ref, ba_ref, xn_ref):
    j = pl.program_id(1)

    @pl.when(j == 0)
    def _():
        half = x_ref.shape[0] // 2
        for r in range(2):
            rows = slice(r * half, (r + 1) * half)
            x = x_ref[rows, :]
            xn = x * lax.rsqrt(jnp.mean(x * x, axis=-1, keepdims=True) + EPS) * gain_ref[...]
            hi, lo = _split2(xn)
            xn_ref[rows, :] = hi
            ba_ref[rows, :] = (jnp.dot(hi, wba_hi_ref[...], preferred_element_type=F32)
                               + jnp.dot(lo, wba_hi_ref[...], preferred_element_type=F32)
                               + jnp.dot(hi, wba_lo_ref[...], preferred_element_type=F32))
            proj_ref[rows, :] = jnp.dot(hi, w_ref[...], preferred_element_type=F32).astype(proj_ref.dtype)

    @pl.when(j != 0)
    def _():
        proj_ref[...] = jnp.dot(xn_ref[...], w_ref[...], preferred_element_type=F32).astype(proj_ref.dtype)


def _inproj(x2, gain, w_main, wba_hi, wba_lo):
    t = x2.shape[0]
    return pl.pallas_call(
        _inproj_kernel,
        grid=(t // IN_TM, MAIN_WIDTH // IN_TN),
        in_specs=[
            pl.BlockSpec((IN_TM, D_MODEL), lambda i, j: (i, 0)),
            pl.BlockSpec((1, D_MODEL), lambda i, j: (0, 0)),
            pl.BlockSpec((D_MODEL, IN_TN), lambda i, j: (0, j)),
            pl.BlockSpec((D_MODEL, LANES), lambda i, j: (0, 0)),
            pl.BlockSpec((D_MODEL, LANES), lambda i, j: (0, 0)),
        ],
        out_specs=[
            pl.BlockSpec((IN_TM, IN_TN), lambda i, j: (i, j)),
            pl.BlockSpec((IN_TM, LANES), lambda i, j: (i, 0)),
        ],
        out_shape=[
            jax.ShapeDtypeStruct((t, MAIN_WIDTH), BF16),
            jax.ShapeDtypeStruct((t, LANES), F32),
        ],
        scratch_shapes=[pltpu.VMEM((IN_TM, D_MODEL), BF16)],
        compiler_params=pltpu.CompilerParams(
            dimension_semantics=("arbitrary", "arbitrary"), vmem_limit_bytes=VMEM_LIMIT),
        name="inproj",
    )(x2, gain, w_main, wba_hi, wba_lo)


def _tri_inverse(ms, row, col):
    blk8 = (row >> 3) == (col >> 3)
    eye = jnp.where(row == col, 1.0, 0.0)
    pds = [jnp.where(blk8, -m, 0.0) for m in ms]
    pdb = [pd.astype(BF16) for pd in pds]
    xs = [eye + pd for pd in pds]
    p2b = [jnp.dot(b, b, preferred_element_type=F32).astype(BF16) for b in pdb]
    yield
    xs = [x + jnp.dot(x.astype(BF16), p2, preferred_element_type=F32) for x, p2 in zip(xs, p2b)]
    yield
    p4b = [jnp.dot(b, b, preferred_element_type=F32).astype(BF16) for b in p2b]
    yield
    xs = [x + jnp.dot(x.astype(BF16), p4, preferred_element_type=F32) for x, p4 in zip(xs, p4b)]
    yield
    for s in (8, 16, 32):
        sh = s.bit_length()
        off = ((row >> sh) == (col >> sh)) & ((row & (2 * s - 1)) >= s) & ((col & (2 * s - 1)) < s)
        xb = [x.astype(BF16) for x in xs]
        ts_ = [jnp.dot(b, jnp.where(off, m, 0.0).astype(BF16), preferred_element_type=F32).astype(BF16)
               for b, m in zip(xb, ms)]
        yield
        xs = [x - jnp.dot(t, b, preferred_element_type=F32) for x, t, b in zip(xs, ts_, xb)]
        yield
    return xs


def _round_robin(*gens):
    live = list(gens)
    while live:
        for g in list(live):
            try:
                next(g)
            except StopIteration:
                live.remove(g)


def _deltanet_kernel(qkv_ref, z_ref, ba_ref, convw_ref, par_ref, gain_ref, ltri_ref, o_ref,
                     buf_ref, act_ref, state_ref, ol_ref, qt_ref, mc_ref, bc_ref, ge_ref):
    ts = qkv_ref.shape[0]
    nqk = DN_HEADS * DN_DK
    heads = range(DN_HEADS)
    nslab = A_QKV // LANES
    nwave = ts // DN_WAVE

    @pl.when(pl.program_id(1) == 0)
    def _():
        buf_ref[:, 0:8, :] = jnp.zeros((nslab, 8, LANES), F32)
        state_ref[...] = jnp.zeros_like(state_ref)

    for s in range(nslab):
        buf_ref[s, 8:8 + ts, :] = qkv_ref[:, s * LANES:(s + 1) * LANES].astype(F32)

    def activations(w):
        for r0 in range(w * DN_WAVE, (w + 1) * DN_WAVE, PAIR):
            for s in range(nslab):
                cs = slice(s * LANES, (s + 1) * LANES)
                acc = None
                for i in range(DN_CONV):
                    lo = 8 - (DN_CONV - 1) + i + r0
                    term = buf_ref[s, lo:lo + PAIR, :] * convw_ref[i:i + 1, cs]
                    acc = term if acc is None else acc + term
                y = _silu(acc)
                if s < 2 * DN_HEADS:
                    y = y * lax.rsqrt(jnp.sum(y * y, axis=-1, keepdims=True) + EPS)
                if s < DN_HEADS:
                    y = y * (DN_DK ** -0.5)
                act_ref[r0:r0 + PAIR, cs] = y
                yield

    ba = ba_ref[...]
    beta_all = jax.nn.sigmoid(ba)
    xs = ba + par_ref[1:2, :]
    softplus = jnp.maximum(xs, 0.0) + jnp.log(1.0 + jnp.exp(-jnp.abs(xs)))
    g_all = -jnp.exp(par_ref[0:1, :]) * softplus
    gc_of = [jnp.dot(ltri_ref[...], jnp.concatenate(_split3(g_all[p * PAIR:(p + 1) * PAIR, :]), axis=0),
                     preferred_element_type=F32) for p in range(ts // PAIR)]

    row = lax.broadcasted_iota(jnp.int32, (PAIR, PAIR), 0)
    col = lax.broadcasted_iota(jnp.int32, (PAIR, PAIR), 1)
    same = (row >> 6) == (col >> 6)
    lower = same & (row >= col)
    strict = same & (row > col)
    rcol = lax.broadcasted_iota(jnp.int32, (PAIR, 1), 0)

    def chunk_local(w):
        pairs = range(w * (DN_WAVE // PAIR), (w + 1) * (DN_WAVE // PAIR))
        units = [(p, h) for p in pairs for h in heads]
        rows_of = [slice(p * PAIR, (p + 1) * PAIR) for p, _ in units]
        gc_t = {p: gc_of[p].T for p in pairs}
        q = [act_ref[rs, h * DN_DK:(h + 1) * DN_DK] for rs, (_, h) in zip(rows_of, units)]
        k = [act_ref[rs, nqk + h * DN_DK:nqk + (h + 1) * DN_DK] for rs, (_, h) in zip(rows_of, units)]
        v = [act_ref[rs, 2 * nqk + h * DN_DV:2 * nqk + (h + 1) * DN_DV] for rs, (_, h) in zip(rows_of, units)]
        beta = [beta_all[rs, h:h + 1] for rs, (_, h) in zip(rows_of, units)]
        gcc = [gc_of[p][:, DN_HEADS + h:DN_HEADS + h + 1] for p, h in units]
        decay = [jnp.where(lower, jnp.exp(g - gc_t[p][DN_HEADS + h:DN_HEADS + h + 1, :]), 0.0)
                 for g, (p, h) in zip(gcc, units)]
        eg = [jnp.exp(g) for g in gcc]
        gend = [jnp.where(rcol < CHUNK, g[CHUNK - 1:CHUNK, :], g[PAIR - 1:PAIR, :]) for g in gcc]
        kb = [x.astype(BF16) for x in k]
        yield
        qkk = [lax.dot_general(jnp.concatenate([qq.astype(BF16), kk], axis=0), kk,
                               (((1,), (1,)), ((), ())), preferred_element_type=F32) for qq, kk in zip(q, kb)]
        yield
        ms = [jnp.where(strict, b * x[PAIR:] * d, 0.0) for b, x, d in zip(beta, qkk, decay)]
        tinv = yield from _tri_inverse(ms, row, col)
        sol = [jnp.dot(t.astype(BF16),
                       jnp.concatenate([(b * vv).astype(BF16), (b * e * kk).astype(BF16)], axis=1),
                       preferred_element_type=F32) for t, b, vv, e, kk in zip(tinv, beta, v, eg, k)]
        yield
        solb = [x.astype(BF16) for x in sol]
        qkb = [jnp.where(lower, x[:PAIR] * d, 0.0).astype(BF16) for x, d in zip(qkk, decay)]
        qwu = [jnp.dot(a, b, preferred_element_type=F32) for a, b in zip(qkb, solb)]
        yield
        kd = [kk * jnp.exp(ge - g) for kk, ge, g in zip(k, gend, gcc)]
        for c in range(PAIR // CHUNK):
            kdc = [jnp.where((row >> 6) == c, x, 0.0).astype(BF16) for x in kd]
            kwu = [lax.dot_general(a, b, (((0,), (0,)), ((), ())), preferred_element_type=F32)
                   for a, b in zip(kdc, solb)]
            yield
            for (p, h), x, g in zip(units, kwu, gcc):
                cc = p * (PAIR // CHUNK) + c
                bc_ref[h, cc] = x[:, :DN_DV]
                mc_ref[h, cc] = x[:, DN_DV:].astype(BF16)
                ge_ref[h, cc] = jnp.broadcast_to(jnp.exp(g[(c + 1) * CHUNK - 1:(c + 1) * CHUNK, :]), (8, LANES))
        for (p, h), rs, x, qq, e in zip(units, rows_of, qwu, q, eg):
            ol_ref[h, rs, :] = x[:, :DN_DV]
            qt_ref[h, rs, :] = (qq * e - x[:, DN_DV:]).astype(BF16)
        yield

    gain = gain_ref[...]
    st = [state_ref[h] for h in heads]

    def recurrence(w):
        for cc in range(w * (DN_WAVE // CHUNK), (w + 1) * (DN_WAVE // CHUNK)):
            cr = slice(cc * CHUNK, (cc + 1) * CHUNK)
            sb = [x.astype(BF16) for x in st]
            o = [ol_ref[h, cr, :] + jnp.dot(qt_ref[h, cr, :], sb[h], preferred_element_type=F32) for h in heads]
            new = [st[h] * ge_ref[h, cc, 0:1, :] + bc_ref[h, cc]
                   - jnp.dot(mc_ref[h, cc], sb[h], preferred_element_type=F32) for h in heads]
            for h in heads:
                st[h] = new[h]
            yield
            for h in heads:
                hs = slice(h * DN_DV, (h + 1) * DN_DV)
                on = o[h] * lax.rsqrt(jnp.mean(o[h] * o[h], axis=-1, keepdims=True) + EPS) * gain
                zz = z_ref[cr, hs].astype(F32)
                o_ref[cr, hs] = (on * _silu(zz)).astype(o_ref.dtype)
            yield

    _round_robin(activations(0))
    for w in range(nwave):
        others = []
        if w + 1 < nwave:
            others.append(activations(w + 1))
        if w > 0:
            others.append(recurrence(w - 1))
        _round_robin(chunk_local(w), *others)
    _round_robin(recurrence(nwave - 1))
    for h in heads:
        state_ref[h] = st[h]
    for s in range(nslab):
        buf_ref[s, 0:8, :] = buf_ref[s, ts:ts + 8, :]


def _deltanet(proj, ba, conv_w, par, gain, ltri3, batch, seq):
    nsb = seq // DN_TS
    nchunk = DN_TS // CHUNK
    return pl.pallas_call(
        _deltanet_kernel,
        grid=(batch, nsb),
        in_specs=[
            pl.BlockSpec((DN_TS, A_QKV), lambda b, s: (b * nsb + s, 0)),
            pl.BlockSpec((DN_TS, DN_WIDTH), lambda b, s: (b * nsb + s, A_QKV // DN_WIDTH)),
            pl.BlockSpec((DN_TS, LANES), lambda b, s: (b * nsb + s, 0)),
            pl.BlockSpec((DN_CONV, A_QKV), lambda b, s: (0, 0)),
            pl.BlockSpec((8, LANES), lambda b, s: (0, 0)),
            pl.BlockSpec((1, DN_DV), lambda b, s: (0, 0)),
            pl.BlockSpec((PAIR, 3 * PAIR), lambda b, s: (0, 0)),
        ],
        out_specs=pl.BlockSpec((DN_TS, DN_WIDTH), lambda b, s: (b * nsb + s, 0)),
        out_shape=jax.ShapeDtypeStruct((batch * seq, DN_WIDTH), BF16),
        scratch_shapes=[
            pltpu.VMEM((A_QKV // LANES, DN_TS + 8, LANES), F32),
            pltpu.VMEM((DN_TS, A_QKV), F32),
            pltpu.VMEM((DN_HEADS, DN_DK, DN_DV), F32),
            pltpu.VMEM((DN_HEADS, DN_TS, DN_DV), F32),
            pltpu.VMEM((DN_HEADS, DN_TS, DN_DK), BF16),
            pltpu.VMEM((DN_HEADS, nchunk, DN_DK, DN_DK), BF16),
            pltpu.VMEM((DN_HEADS, nchunk, DN_DK, DN_DV), F32),
            pltpu.VMEM((DN_HEADS, nchunk, 8, LANES), F32),
        ],
        compiler_params=pltpu.CompilerParams(
            dimension_semantics=("arbitrary", "arbitrary"), vmem_limit_bytes=VMEM_LIMIT),
        name="deltanet",
    )(proj, proj, ba, conv_w, par, gain, ltri3)


def _rms_halves(x, half_ones2):
    hi, lo = _split2(x * x)
    ss = jnp.dot(jnp.concatenate([hi, lo], axis=1), half_ones2, preferred_element_type=F32)
    return x * lax.rsqrt(ss * (1.0 / SB_DH) + EPS)


def _sb_kernel(q_ref, k_ref, v_ref, qg_ref, kg_ref, cum_ref, half_ref, o_ref,
               q2_ref, kn_ref, v2_ref, r_ref, acc_ref, flag_ref):
    i = pl.program_id(1)
    nblk = k_ref.shape[0] // SB_BLOCK
    npair = k_ref.shape[1] // LANES
    pairs = range(npair)
    lane = lax.broadcasted_iota(jnp.int32, (SB_BLOCK, LANES), 1)
    first = lane < SB_DH

    def cols(p):
        return slice(p * LANES, (p + 1) * LANES)

    @pl.when(i == 0)
    def _():
        qscale = qg_ref[...] * (SB_DH ** -0.5 * LOG2E)

        def prep(j, carry):
            rows = pl.ds(pl.multiple_of(j * SB_BLOCK, SB_BLOCK), SB_BLOCK)
            for p in pairs:
                qn = (_rms_halves(q_ref[rows, cols(p)].astype(F32), half_ref[...]) * qscale).astype(BF16)
                zero = jnp.zeros_like(qn)
                q2_ref[p, j, 0:SB_BLOCK, :] = jnp.where(first, qn, zero)
                q2_ref[p, j, SB_BLOCK:2 * SB_BLOCK, :] = jnp.where(first, zero, qn)
                kn = _rms_halves(k_ref[rows, cols(p)].astype(F32), half_ref[...]) * kg_ref[...]
                kn_ref[rows, cols(p)] = kn.astype(BF16)
                vb = v_ref[rows, cols(p)]
                v2_ref[p, j, 0:SB_BLOCK, :] = jnp.where(first, vb, zero)
                v2_ref[p, j, SB_BLOCK:2 * SB_BLOCK, :] = jnp.where(first, zero, vb)
            return carry
        lax.fori_loop(0, nblk, prep, 0)

    q2 = [q2_ref[p, i] for p in pairs]

    def scores(rows, mask):
        zs = [lax.dot_general(q2[p], kn_ref[rows, cols(p)], (((1,), (1,)), ((), ())),
                              preferred_element_type=F32) for p in pairs]
        lss = [jnp.minimum(z, 0.0) - jnp.log(1.0 + jnp.exp2(-jnp.abs(z))) * LOG2E for z in zs]
        lks = [ls - z for ls, z in zip(lss, zs)]
        if mask is not None:
            lks = [jnp.where(mask, lk, 0.0) for lk in lks]
        return lss, [jnp.concatenate(_split2(lk), axis=1) for lk in lks]

    def cumsums(lk2s):
        return [jnp.dot(lk2, cum_ref[...], preferred_element_type=F32) for lk2 in lk2s]

    def weights(lss, css, rs, mask):
        if rs is None:
            avs = [jnp.exp2(ls + cs[:, :SB_BLOCK]) for ls, cs in zip(lss, css)]
        else:
            avs = [jnp.exp2(ls + cs[:, :SB_BLOCK] + r) for ls, cs, r in zip(lss, css, rs)]
        if mask is not None:
            avs = [jnp.where(mask, a, 0.0) for a in avs]
        out = []
        for a in avs:
            ab = a.astype(BF16)
            out.append(jnp.concatenate([ab[:SB_BLOCK], ab[SB_BLOCK:]], axis=1))
        return out

    def tile_rows(j):
        return pl.ds(pl.multiple_of(j * SB_BLOCK, SB_BLOCK), SB_BLOCK)

    def sweep(tiles, masks, rs, accs):
        sc = [scores(tile_rows(j), m) for j, m in zip(tiles, masks)]
        css = [cumsums(lk2s) for _, lk2s in sc]
        for j, m, (lss, _), cs in zip(tiles, masks, sc, css):
            a2s = weights(lss, cs, rs, m)
            rs = [c[:, SB_BLOCK:] for c in cs] if rs is None else [r + c[:, SB_BLOCK:] for r, c in zip(rs, cs)]
            accs = [acc + jnp.dot(a2, v2_ref[p, j], preferred_element_type=F32)
                    for p, (acc, a2) in enumerate(zip(accs, a2s))]
        return rs, accs

    def load_state():
        return [r_ref[p] for p in pairs], [acc_ref[:, cols(p)] for p in pairs]

    def store_state(rs, accs):
        for p in pairs:
            r_ref[p] = rs[p]
            acc_ref[:, cols(p)] = accs[p]

    qrow = lax.broadcasted_iota(jnp.int32, (2 * SB_BLOCK, SB_BLOCK), 0) & (SB_BLOCK - 1)
    kcol = lax.broadcasted_iota(jnp.int32, (2 * SB_BLOCK, SB_BLOCK), 1)
    causal = kcol < qrow
    zero_accs = [jnp.zeros((SB_BLOCK, LANES), F32)] * npair

    def alive_flag(rs):
        return (jnp.max(functools.reduce(jnp.maximum, rs)) >= ZERO_WEIGHT_LOG2).astype(jnp.int32)

    for n in range(1, SB_LEAD + 1):
        @pl.when((i == n - 1) if n < SB_LEAD else (i >= n - 1))
        def _(n=n):
            rs, accs = sweep([i - d for d in range(n)], [causal] + [None] * (n - 1), None, zero_accs)
            store_state(rs, accs)
            flag_ref[0] = alive_flag(rs)

    def cond(carry):
        t, alive = carry
        return jnp.logical_and(t < i, alive > 0)

    def body(carry):
        t, _ = carry
        j = i - 1 - t
        rs, accs = sweep([j], [None], *load_state())
        store_state(rs, accs)
        return t + 1, alive_flag(rs)
    lax.while_loop(cond, body, (jnp.int32(SB_LEAD - 1), flag_ref[0]))

    o_ref[...] = acc_ref[...].astype(o_ref.dtype)


def _stick_breaking(proj, qg2, kg2, cum_mat, half_ones2, batch, seq):
    nq = seq // SB_BLOCK
    npair = SB_HEADS // 2
    qcol = (A_QKV + DN_WIDTH) // SB_WIDTH
    return pl.pallas_call(
        _sb_kernel,
        grid=(batch, nq),
        in_specs=[
            pl.BlockSpec((seq, SB_WIDTH), lambda b, i: (b, qcol)),
            pl.BlockSpec((seq, SB_WIDTH), lambda b, i: (b, qcol + 1)),
            pl.BlockSpec((seq, SB_WIDTH), lambda b, i: (b, qcol + 2)),
            pl.BlockSpec((1, LANES), lambda b, i: (0, 0)),
            pl.BlockSpec((1, LANES), lambda b, i: (0, 0)),
            pl.BlockSpec((2 * SB_BLOCK, 2 * SB_BLOCK), lambda b, i: (0, 0)),
            pl.BlockSpec((2 * LANES, LANES), lambda b, i: (0, 0)),
        ],
        out_specs=pl.BlockSpec((SB_BLOCK, SB_WIDTH), lambda b, i: (b * nq + i, 0)),
        out_shape=jax.ShapeDtypeStruct((batch * seq, SB_WIDTH), BF16),
        scratch_shapes=[
            pltpu.VMEM((npair, nq, 2 * SB_BLOCK, LANES), BF16),
            pltpu.VMEM((seq, SB_WIDTH), BF16),
            pltpu.VMEM((npair, nq, 2 * SB_BLOCK, LANES), BF16),
            pltpu.VMEM((npair, 2 * SB_BLOCK, SB_BLOCK), F32),
            pltpu.VMEM((SB_BLOCK, SB_WIDTH), F32),
            pltpu.SMEM((1,), jnp.int32),
        ],
        compiler_params=pltpu.CompilerParams(
            dimension_semantics=("arbitrary", "arbitrary"), vmem_limit_bytes=VMEM_LIMIT),
        name="stickbreak",
    )(proj, proj, proj, qg2, kg2, cum_mat, half_ones2)


def _merge_kernel(x_ref, gain_ref, oa_ref, ob_ref, zb_ref, wg_ref, bg_ref, wa_ref, wb_ref, wo_ref, out_ref):
    x = x_ref[...]
    xn = (x * lax.rsqrt(jnp.mean(x * x, axis=-1, keepdims=True) + EPS) * gain_ref[...]).astype(BF16)
    gates = jax.nn.sigmoid(jnp.dot(xn, wg_ref[...], preferred_element_type=F32) + bg_ref[...])
    zb = zb_ref[...].astype(F32)
    ob = (ob_ref[...].astype(F32) * _silu(zb)).astype(BF16)
    ya = jnp.dot(oa_ref[...], wa_ref[...], preferred_element_type=F32)
    yb = jnp.dot(ob, wb_ref[...], preferred_element_type=F32)
    merged = gates[:, :D_MODEL] * ya + gates[:, D_MODEL:] * yb
    out_ref[...] = x + jnp.dot(merged.astype(BF16), wo_ref[...], preferred_element_type=F32)


def _merge(x2, gain, o_a, o_b, proj, w_gate, b_gate, w_up_a, w_up_b, w_out):
    t = x2.shape[0]
    zb_col = (A_QKV + DN_WIDTH + 3 * SB_WIDTH) // SB_WIDTH
    const = lambda i: (0, 0)
    return pl.pallas_call(
        _merge_kernel,
        grid=(t // OUT_TM,),
        in_specs=[
            pl.BlockSpec((OUT_TM, D_MODEL), lambda i: (i, 0)),
            pl.BlockSpec((1, D_MODEL), const),
            pl.BlockSpec((OUT_TM, DN_WIDTH), lambda i: (i, 0)),
            pl.BlockSpec((OUT_TM, SB_WIDTH), lambda i: (i, 0)),
            pl.BlockSpec((OUT_TM, SB_WIDTH), lambda i: (i, zb_col)),
            pl.BlockSpec((D_MODEL, 2 * D_MODEL), const),
            pl.BlockSpec((1, 2 * D_MODEL), const),
            pl.BlockSpec((DN_WIDTH, D_MODEL), const),
            pl.BlockSpec((SB_WIDTH, D_MODEL), const),
            pl.BlockSpec((D_MODEL, D_MODEL), const),
        ],
        out_specs=pl.BlockSpec((OUT_TM, D_MODEL), lambda i: (i, 0)),
        out_shape=jax.ShapeDtypeStruct((t, D_MODEL), F32),
        compiler_params=pltpu.CompilerParams(
            dimension_semantics=("arbitrary",), vmem_limit_bytes=VMEM_LIMIT),
        name="merge",
    )(x2, gain, o_a, o_b, proj, w_gate, b_gate, w_up_a, w_up_b, w_out)


def _layer(x, norm_gain, w_in, b_gate, conv_w, a_log, dt_bias, dn_out_gain,
           sb_q_gain, sb_k_gain, w_up_a, w_up_b, w_out):
    batch, seq, _ = x.shape
    x2 = x.reshape(batch * seq, D_MODEL)

    w_main = jnp.concatenate([w_in[:, :BA_COL0], w_in[:, BA_COL0 + BA_COLS:GATE_COL0]], axis=1).astype(BF16)
    w_gate = w_in[:, GATE_COL0:].astype(BF16)
    w_ba = jnp.pad(w_in[:, BA_COL0:BA_COL0 + BA_COLS], ((0, 0), (0, LANES - BA_COLS)))
    wba_hi = w_ba.astype(BF16)
    wba_lo = (w_ba - wba_hi.astype(F32)).astype(BF16)
    par = jnp.zeros((8, LANES), F32)
    par = par.at[0, DN_HEADS:2 * DN_HEADS].set(a_log).at[1, DN_HEADS:2 * DN_HEADS].set(dt_bias)
    qg2 = jnp.concatenate([sb_q_gain, sb_q_gain])[None, :]
    kg2 = jnp.concatenate([sb_k_gain, sb_k_gain])[None, :]
    kk = jnp.arange(SB_BLOCK)
    cum1 = jnp.concatenate([(kk[:, None] > kk[None, :]).astype(BF16), jnp.ones((SB_BLOCK, SB_BLOCK), BF16)], axis=1)
    cum_mat = jnp.concatenate([cum1, cum1], axis=0)
    ln = jnp.arange(LANES)
    half1 = ((ln[:, None] // SB_DH) == (ln[None, :] // SB_DH)).astype(BF16)
    half_ones2 = jnp.concatenate([half1, half1], axis=0)
    tt = jnp.arange(PAIR)
    ltri1 = ((tt[:, None] // CHUNK == tt[None, :] // CHUNK) & (tt[:, None] >= tt[None, :])).astype(BF16)
    ltri3 = jnp.concatenate([ltri1, ltri1, ltri1], axis=1)

    proj, ba = _inproj(x2, norm_gain[None, :], w_main, wba_hi, wba_lo)
    o_a = _deltanet(proj, ba, conv_w, par, dn_out_gain[None, :], ltri3, batch, seq)
    o_b = _stick_breaking(proj, qg2, kg2, cum_mat, half_ones2, batch, seq)
    out = _merge(x2, norm_gain[None, :], o_a, o_b, proj, w_gate, b_gate[None, :], w_up_a.astype(BF16),
                 w_up_b.astype(BF16), w_out.astype(BF16))
    return out.reshape(batch, seq, D_MODEL)


def kernel(x, norm_gain, w_in, b_gate, conv_w, a_log, dt_bias, dn_out_gain, sb_q_gain, sb_k_gain,
           w_up_a, w_up_b, w_out):
    h = x
    for layer in range(norm_gain.shape[0]):
        h = _layer(h, norm_gain[layer], w_in[layer], b_gate[layer], conv_w[layer], a_log[layer],
                   dt_bias[layer], dn_out_gain[layer], sb_q_gain[layer], sb_k_gain[layer],
                   w_up_a[layer], w_up_b[layer], w_out[layer])
    return h
```

```python
import functools

import jax
import jax.numpy as jnp
from jax import lax
from jax.experimental import pallas as pl
from jax.experimental.pallas import tpu as pltpu

F32 = jnp.float32
BF16 = jnp.bfloat16

D_MODEL = 1024
CHUNK = 64
SB_BLOCK = 128
EPS = 1e-6
LOG2E = 1.4426950408889634
ZERO_WEIGHT_LOG2 = -151.0
SB_LEAD = 3
SB_NARROW = 32
DN_HEADS = 4
DN_DK = 128
DN_DV = 128
DN_CONV = 4
SB_HEADS = 8
SB_DH = 64
DN_WIDTH = DN_HEADS * DN_DV
SB_WIDTH = SB_HEADS * SB_DH
A_QKV = 2 * DN_HEADS * DN_DK + DN_WIDTH
BA_COL0 = A_QKV + DN_WIDTH
BA_COLS = 2 * DN_HEADS
MAIN_WIDTH = A_QKV + DN_WIDTH + 3 * SB_WIDTH + SB_WIDTH
GATE_COL0 = BA_COL0 + BA_COLS + 4 * SB_WIDTH

LANES = 128
PAIR = 2 * CHUNK

IN_TM, IN_TN = 1024, 2048
DN_TS = 1024
DN_WAVE = 256
OUT_TM = 512

VMEM_LIMIT = 48 * 1024 * 1024


def _split3(x):
    hi = x.astype(BF16)
    r1 = x - hi.astype(F32)
    mid = r1.astype(BF16)
    lo = (r1 - mid.astype(F32)).astype(BF16)
    return hi, mid, lo


def _split2(x):
    hi = x.astype(BF16)
    lo = (x - hi.astype(F32)).astype(BF16)
    return hi, lo


def _silu(x):
    h = 0.5 * x
    return h + h * jnp.tanh(h)


def _inproj_kernel(x_ref, gain_ref, w_ref, wba_hi_ref, wba_lo_ref, proj_ref, ba_ref, xn_ref):
    j = pl.program_id(1)

    @pl.when(j == 0)
    def _():
        half = x_ref.shape[0] // 2
        for r in range(2):
            rows = slice(r * half, (r + 1) * half)
            x = x_ref[rows, :]
            xn = x * lax.rsqrt(jnp.mean(x * x, axis=-1, keepdims=True) + EPS) * gain_ref[...]
            hi, lo = _split2(xn)
            xn_ref[rows, :] = hi
            ba_ref[rows, :] = (jnp.dot(hi, wba_hi_ref[...], preferred_element_type=F32)
                               + jnp.dot(lo, wba_hi_ref[...], preferred_element_type=F32)
                               + jnp.dot(hi, wba_lo_ref[...], preferred_element_type=F32))
            proj_ref[rows, :] = jnp.dot(hi, w_ref[...], preferred_element_type=F32).astype(proj_ref.dtype)

    @pl.when(j != 0)
    def _():
        proj_ref[...] = jnp.dot(xn_ref[...], w_ref[...], preferred_element_type=F32).astype(proj_ref.dtype)


def _inproj(x2, gain, w_main, wba_hi, wba_lo):
    t = x2.shape[0]
    return pl.pallas_call(
        _inproj_kernel,
        grid=(t // IN_TM, MAIN_WIDTH // IN_TN),
        in_specs=[
            pl.BlockSpec((IN_TM, D_MODEL), lambda i, j: (i, 0)),
            pl.BlockSpec((1, D_MODEL), lambda i, j: (0, 0)),
            pl.BlockSpec((D_MODEL, IN_TN), lambda i, j: (0, j)),
            pl.BlockSpec((D_MODEL, LANES), lambda i, j: (0, 0)),
            pl.BlockSpec((D_MODEL, LANES), lambda i, j: (0, 0)),
        ],
        out_specs=[
            pl.BlockSpec((IN_TM, IN_TN), lambda i, j: (i, j)),
            pl.BlockSpec((IN_TM, LANES), lambda i, j: (i, 0)),
        ],
        out_shape=[
            jax.ShapeDtypeStruct((t, MAIN_WIDTH), BF16),
            jax.ShapeDtypeStruct((t, LANES), F32),
        ],
        scratch_shapes=[pltpu.VMEM((IN_TM, D_MODEL), BF16)],
        compiler_params=pltpu.CompilerParams(
            dimension_semantics=("arbitrary", "arbitrary"), vmem_limit_bytes=VMEM_LIMIT),
        name="inproj",
    )(x2, gain, w_main, wba_hi, wba_lo)


def _tri_inverse(ms, row, col):
    blk8 = (row >> 3) == (col >> 3)
    eye = jnp.where(row == col, 1.0, 0.0)
    pds = [jnp.where(blk8, -m, 0.0) for m in ms]
    pdb = [pd.astype(BF16) for pd in pds]
    xs = [eye + pd for pd in pds]
    p2b = [jnp.dot(b, b, preferred_element_type=F32).astype(BF16) for b in pdb]
    yield
    xs = [x + jnp.dot(x.astype(BF16), p2, preferred_element_type=F32) for x, p2 in zip(xs, p2b)]
    yield
    p4b = [jnp.dot(b, b, preferred_element_type=F32).astype(BF16) for b in p2b]
    yield
    xs = [x + jnp.dot(x.astype(BF16), p4, preferred_element_type=F32) for x, p4 in zip(xs, p4b)]
    yield
    for s in (8, 16, 32):
        sh = s.bit_length()
        off = ((row >> sh) == (col >> sh)) & ((row & (2 * s - 1)) >= s) & ((col & (2 * s - 1)) < s)
        xb = [x.astype(BF16) for x in xs]
        ts_ = [jnp.dot(b, jnp.where(off, m, 0.0).astype(BF16), preferred_element_type=F32).astype(BF16)
               for b, m in zip(xb, ms)]
        yield
        xs = [x - jnp.dot(t, b, preferred_element_type=F32) for x, t, b in zip(xs, ts_, xb)]
        yield
    return xs


def _round_robin(*gens):
    live = list(gens)
    while live:
        for g in list(live):
            try:
                next(g)
            except StopIteration:
                live.remove(g)


def _deltanet_kernel(qkv_ref, z_ref, ba_ref, convw_ref, par_ref, gain_ref, ltri_ref, o_ref,
                     buf_ref, act_ref, state_ref, ol_ref, qt_ref, mc_ref, bc_ref, ge_ref):
    ts = qkv_ref.shape[0]
    nqk = DN_HEADS * DN_DK
    heads = range(DN_HEADS)
    nslab = A_QKV // LANES
    nwave = ts // DN_WAVE

    @pl.when(pl.program_id(1) == 0)
    def _():
        buf_ref[:, 0:8, :] = jnp.zeros((nslab, 8, LANES), F32)
        state_ref[...] = jnp.zeros_like(state_ref)

    for s in range(nslab):
        buf_ref[s, 8:8 + ts, :] = qkv_ref[:, s * LANES:(s + 1) * LANES].astype(F32)

    def activations(w):
        for r0 in range(w * DN_WAVE, (w + 1) * DN_WAVE, PAIR):
            for s in range(nslab):
                cs = slice(s * LANES, (s + 1) * LANES)
                acc = None
                for i in range(DN_CONV):
                    lo = 8 - (DN_CONV - 1) + i + r0
                    term = buf_ref[s, lo:lo + PAIR, :] * convw_ref[i:i + 1, cs]
                    acc = term if acc is None else acc + term
                y = _silu(acc)
                if s < 2 * DN_HEADS:
                    y = y * lax.rsqrt(jnp.sum(y * y, axis=-1, keepdims=True) + EPS)
                if s < DN_HEADS:
                    y = y * (DN_DK ** -0.5)
                act_ref[r0:r0 + PAIR, cs] = y
                yield

    ba = ba_ref[...]
    beta_all = jax.nn.sigmoid(ba)
    xs = ba + par_ref[1:2, :]
    softplus = jnp.maximum(xs, 0.0) + jnp.log(1.0 + jnp.exp(-jnp.abs(xs)))
    g_all = -jnp.exp(par_ref[0:1, :]) * softplus
    gc_of = [jnp.dot(ltri_ref[...], jnp.concatenate(_split3(g_all[p * PAIR:(p + 1) * PAIR, :]), axis=0),
                     preferred_element_type=F32) for p in range(ts // PAIR)]

    row = lax.broadcasted_iota(jnp.int32, (PAIR, PAIR), 0)
    col = lax.broadcasted_iota(jnp.int32, (PAIR, PAIR), 1)
    same = (row >> 6) == (col >> 6)
    lower = same & (row >= col)
    strict = same & (row > col)
    rcol = lax.broadcasted_iota(jnp.int32, (PAIR, 1), 0)

    def chunk_local(w):
        pairs = range(w * (DN_WAVE // PAIR), (w + 1) * (DN_WAVE // PAIR))
        units = [(p, h) for p in pairs for h in heads]
        rows_of = [slice(p * PAIR, (p + 1) * PAIR) for p, _ in units]
        gc_t = {p: gc_of[p].T for p in pairs}
        q = [act_ref[rs, h * DN_DK:(h + 1) * DN_DK] for rs, (_, h) in zip(rows_of, units)]
        k = [act_ref[rs, nqk + h * DN_DK:nqk + (h + 1) * DN_DK] for rs, (_, h) in zip(rows_of, units)]
        v = [act_ref[rs, 2 * nqk + h * DN_DV:2 * nqk + (h + 1) * DN_DV] for rs, (_, h) in zip(rows_of, units)]
        beta = [beta_all[rs, h:h + 1] for rs, (_, h) in zip(rows_of, units)]
        gcc = [gc_of[p][:, DN_HEADS + h:DN_HEADS + h + 1] for p, h in units]
        decay = [jnp.where(lower, jnp.exp(g - gc_t[p][DN_HEADS + h:DN_HEADS + h + 1, :]), 0.0)
                 for g, (p, h) in zip(gcc, units)]
        eg = [jnp.exp(g) for g in gcc]
        gend = [jnp.where(rcol < CHUNK, g[CHUNK - 1:CHUNK, :], g[PAIR - 1:PAIR, :]) for g in gcc]
        kb = [x.astype(BF16) for x in k]
        yield
        qkk = [lax.dot_general(jnp.concatenate([qq.astype(BF16), kk], axis=0), kk,
                               (((1,), (1,)), ((), ())), preferred_element_type=F32) for qq, kk in zip(q, kb)]
        yield
        ms = [jnp.where(strict, b * x[PAIR:] * d, 0.0) for b, x, d in zip(beta, qkk, decay)]
        tinv = yield from _tri_inverse(ms, row, col)
        sol = [jnp.dot(t.astype(BF16),
                       jnp.concatenate([(b * vv).astype(BF16), (b * e * kk).astype(BF16)], axis=1),
                       preferred_element_type=F32) for t, b, vv, e, kk in zip(tinv, beta, v, eg, k)]
        yield
        solb = [x.astype(BF16) for x in sol]
        qkb = [jnp.where(lower, x[:PAIR] * d, 0.0).astype(BF16) for x, d in zip(qkk, decay)]
        qwu = [jnp.dot(a, b, preferred_element_type=F32) for a, b in zip(qkb, solb)]
        yield
        kd = [kk * jnp.exp(ge - g) for kk, ge, g in zip(k, gend, gcc)]
        for c in range(PAIR // CHUNK):
            kdc = [jnp.where((row >> 6) == c, x, 0.0).astype(BF16) for x in kd]
            kwu = [lax.dot_general(a, b, (((0,), (0,)), ((), ())), preferred_element_type=F32)
                   for a, b in zip(kdc, solb)]
            yield
            for (p, h), x, g in zip(units, kwu, gcc):
                cc = p * (PAIR // CHUNK) + c
                bc_ref[h, cc] = x[:, :DN_DV]
                mc_ref[h, cc] = x[:, DN_DV:].astype(BF16)
                ge_ref[h, cc] = jnp.broadcast_to(jnp.exp(g[(c + 1) * CHUNK - 1:(c + 1) * CHUNK, :]), (8, LANES))
        for (p, h), rs, x, qq, e in zip(units, rows_of, qwu, q, eg):
            ol_ref[h, rs, :] = x[:, :DN_DV]
            qt_ref[h, rs, :] = (qq * e - x[:, DN_DV:]).astype(BF16)
        yield

    gain = gain_ref[...]
    st = [state_ref[h] for h in heads]

    def recurrence(w):
        for cc in range(w * (DN_WAVE // CHUNK), (w + 1) * (DN_WAVE // CHUNK)):
            cr = slice(cc * CHUNK, (cc + 1) * CHUNK)
            sb = [x.astype(BF16) for x in st]
            o = [ol_ref[h, cr, :] + jnp.dot(qt_ref[h, cr, :], sb[h], preferred_element_type=F32) for h in heads]
            new = [st[h] * ge_ref[h, cc, 0:1, :] + bc_ref[h, cc]
                   - jnp.dot(mc_ref[h, cc], sb[h], preferred_element_type=F32) for h in heads]
            for h in heads:
                st[h] = new[h]
            yield
            for h in heads:
                hs = slice(h * DN_DV, (h + 1) * DN_DV)
                on = o[h] * lax.rsqrt(jnp.mean(o[h] * o[h], axis=-1, keepdims=True) + EPS) * gain
                zz = z_ref[cr, hs].astype(F32)
                o_ref[cr, hs] = (on * _silu(zz)).astype(o_ref.dtype)
            yield

    _round_robin(activations(0))
    for w in range(nwave):
        others = []
        if w + 1 < nwave:
            others.append(activations(w + 1))
        if w > 0:
            others.append(recurrence(w - 1))
        _round_robin(chunk_local(w), *others)
    _round_robin(recurrence(nwave - 1))
    for h in heads:
        state_ref[h] = st[h]
    for s in range(nslab):
        buf_ref[s, 0:8, :] = buf_ref[s, ts:ts + 8, :]


def _deltanet(proj, ba, conv_w, par, gain, ltri3, batch, seq):
    nsb = seq // DN_TS
    nchunk = DN_TS // CHUNK
    return pl.pallas_call(
        _deltanet_kernel,
        grid=(batch, nsb),
        in_specs=[
            pl.BlockSpec((DN_TS, A_QKV), lambda b, s: (b * nsb + s, 0)),
            pl.BlockSpec((DN_TS, DN_WIDTH), lambda b, s: (b * nsb + s, A_QKV // DN_WIDTH)),
            pl.BlockSpec((DN_TS, LANES), lambda b, s: (b * nsb + s, 0)),
            pl.BlockSpec((DN_CONV, A_QKV), lambda b, s: (0, 0)),
            pl.BlockSpec((8, LANES), lambda b, s: (0, 0)),
            pl.BlockSpec((1, DN_DV), lambda b, s: (0, 0)),
            pl.BlockSpec((PAIR, 3 * PAIR), lambda b, s: (0, 0)),
        ],
        out_specs=pl.BlockSpec((DN_TS, DN_WIDTH), lambda b, s: (b * nsb + s, 0)),
        out_shape=jax.ShapeDtypeStruct((batch * seq, DN_WIDTH), BF16),
        scratch_shapes=[
            pltpu.VMEM((A_QKV // LANES, DN_TS + 8, LANES), F32),
            pltpu.VMEM((DN_TS, A_QKV), F32),
            pltpu.VMEM((DN_HEADS, DN_DK, DN_DV), F32),
            pltpu.VMEM((DN_HEADS, DN_TS, DN_DV), F32),
            pltpu.VMEM((DN_HEADS, DN_TS, DN_DK), BF16),
            pltpu.VMEM((DN_HEADS, nchunk, DN_DK, DN_DK), BF16),
            pltpu.VMEM((DN_HEADS, nchunk, DN_DK, DN_DV), F32),
            pltpu.VMEM((DN_HEADS, nchunk, 8, LANES), F32),
        ],
        compiler_params=pltpu.CompilerParams(
            dimension_semantics=("arbitrary", "arbitrary"), vmem_limit_bytes=VMEM_LIMIT),
        name="deltanet",
    )(proj, proj, ba, conv_w, par, gain, ltri3)


def _rms_halves(x, half_ones2):
    hi, lo = _split2(x * x)
    ss = jnp.dot(jnp.concatenate([hi, lo], axis=1), half_ones2, preferred_element_type=F32)
    return x * lax.rsqrt(ss * (1.0 / SB_DH) + EPS)


def _sb_kernel(q_ref, k_ref, v_ref, qg_ref, kg_ref, cum_ref, half_ref, o_ref,
               q2_ref, kn_ref, v2_ref, r_ref, acc_ref, flag_ref):
    i = pl.program_id(1)
    nblk = k_ref.shape[0] // SB_BLOCK
    npair = k_ref.shape[1] // LANES
    pairs = range(npair)
    lane = lax.broadcasted_iota(jnp.int32, (SB_BLOCK, LANES), 1)
    first = lane < SB_DH

    def cols(p):
        return slice(p * LANES, (p + 1) * LANES)

    @pl.when(i == 0)
    def _():
        qscale = qg_ref[...] * (SB_DH ** -0.5 * LOG2E)

        def prep(j, carry):
            rows = pl.ds(pl.multiple_of(j * SB_BLOCK, SB_BLOCK), SB_BLOCK)
            for p in pairs:
                qn = (_rms_halves(q_ref[rows, cols(p)].astype(F32), half_ref[...]) * qscale).astype(BF16)
                zero = jnp.zeros_like(qn)
                q2_ref[p, j, 0:SB_BLOCK, :] = jnp.where(first, qn, zero)
                q2_ref[p, j, SB_BLOCK:2 * SB_BLOCK, :] = jnp.where(first, zero, qn)
                kn = _rms_halves(k_ref[rows, cols(p)].astype(F32), half_ref[...]) * kg_ref[...]
                kn_ref[rows, cols(p)] = kn.astype(BF16)
                vb = v_ref[rows, cols(p)]
                v2_ref[p, j, 0:SB_BLOCK, :] = jnp.where(first, vb, zero)
                v2_ref[p, j, SB_BLOCK:2 * SB_BLOCK, :] = jnp.where(first, zero, vb)
            return carry
        lax.fori_loop(0, nblk, prep, 0)

    q2 = [q2_ref[p, i] for p in pairs]

    def scores(rows, mask):
        zs = [lax.dot_general(q2[p], kn_ref[rows, cols(p)], (((1,), (1,)), ((), ())),
                              preferred_element_type=F32) for p in pairs]
        lss = [jnp.minimum(z, 0.0) - jnp.log(1.0 + jnp.exp2(-jnp.abs(z))) * LOG2E for z in zs]
        lks = [ls - z for ls, z in zip(lss, zs)]
        if mask is not None:
            lks = [jnp.where(mask, lk, 0.0) for lk in lks]
        return lss, [jnp.concatenate(_split2(lk), axis=1) for lk in lks]

    def cumsums(lk2s):
        return [jnp.dot(lk2, cum_ref[...], preferred_element_type=F32) for lk2 in lk2s]

    def weights(lss, css, rs, mask):
        if rs is None:
            avs = [jnp.exp2(ls + cs[:, :SB_BLOCK]) for ls, cs in zip(lss, css)]
        else:
            avs = [jnp.exp2(ls + cs[:, :SB_BLOCK] + r) for ls, cs, r in zip(lss, css, rs)]
        if mask is not None:
            avs = [jnp.where(mask, a, 0.0) for a in avs]
        out = []
        for a in avs:
            ab = a.astype(BF16)
            out.append(jnp.concatenate([ab[:SB_BLOCK], ab[SB_BLOCK:]], axis=1))
        return out

    def tile_rows(j):
        return pl.ds(pl.multiple_of(j * SB_BLOCK, SB_BLOCK), SB_BLOCK)

    def sweep(tiles, masks, rs, accs):
        sc = [scores(tile_rows(j), m) for j, m in zip(tiles, masks)]
        css = [cumsums(lk2s) for _, lk2s in sc]
        for j, m, (lss, _), cs in zip(tiles, masks, sc, css):
            a2s = weights(lss, cs, rs, m)
            rs = [c[:, SB_BLOCK:] for c in cs] if rs is None else [r + c[:, SB_BLOCK:] for r, c in zip(rs, cs)]
            accs = [acc + jnp.dot(a2, v2_ref[p, j], preferred_element_type=F32)
                    for p, (acc, a2) in enumerate(zip(accs, a2s))]
        return rs, accs

    def load_state():
        return [r_ref[p] for p in pairs], [acc_ref[:, cols(p)] for p in pairs]

    def store_state(rs, accs):
        for p in pairs:
            r_ref[p] = rs[p]
            acc_ref[:, cols(p)] = accs[p]

    qrow = lax.broadcasted_iota(jnp.int32, (2 * SB_BLOCK, SB_BLOCK), 0) & (SB_BLOCK - 1)
    kcol = lax.broadcasted_iota(jnp.int32, (2 * SB_BLOCK, SB_BLOCK), 1)
    causal = kcol < qrow
    zero_accs = [jnp.zeros((SB_BLOCK, LANES), F32)] * npair

    def alive_flag(rs):
        return (jnp.max(functools.reduce(jnp.maximum, rs)) >= ZERO_WEIGHT_LOG2).astype(jnp.int32)

    def head_rows(x, lo, hi):
        return jnp.concatenate([x[lo:hi], x[SB_BLOCK + lo:SB_BLOCK + hi]], axis=0)

    @pl.when(i == 0)
    def _():
        rs, accs = sweep([i], [causal], None, zero_accs)
        store_state(rs, accs)
        flag_ref[0] = alive_flag(rs)
        flag_ref[1] = jnp.int32(0)

    @pl.when(i == 1)
    def _():
        rs, accs = sweep([i, i - 1], [causal, None], None, zero_accs)
        store_state(rs, accs)
        flag_ref[0] = alive_flag(rs)
        flag_ref[1] = jnp.int32(0)

    @pl.when(i >= 2)
    def _():
        j3 = i - 2
        sc = [scores(tile_rows(j), m) for j, m in ((i, causal), (i - 1, None))]
        q2n = [head_rows(q, 0, SB_NARROW) for q in q2]
        zs = [lax.dot_general(q2n[p], kn_ref[tile_rows(j3), cols(p)], (((1,), (1,)), ((), ())),
                              preferred_element_type=F32) for p in pairs]
        lss_n = [jnp.minimum(z, 0.0) - jnp.log(1.0 + jnp.exp2(-jnp.abs(z))) * LOG2E for z in zs]
        lk2_n = [jnp.concatenate(_split2(ls - z), axis=1) for ls, z in zip(lss_n, zs)]
        css = [cumsums(lk2s) for _, lk2s in sc]
        css_n = cumsums(lk2_n)
        rs, accs = None, zero_accs
        for j, m, (lss, _), cs in zip((i, i - 1), (causal, None), sc, css):
            a2s = weights(lss, cs, rs, m)
            rs = [c[:, SB_BLOCK:] for c in cs] if rs is None else [r + c[:, SB_BLOCK:] for r, c in zip(rs, cs)]
            accs = [acc + jnp.dot(a2, v2_ref[p, j], preferred_element_type=F32)
                    for p, (acc, a2) in enumerate(zip(accs, a2s))]
        rest = [head_rows(r, SB_NARROW, SB_BLOCK) for r in rs]
        rs_n = [head_rows(r, 0, SB_NARROW) for r in rs]
        a_n = [jnp.exp2(ls + cs[:, :SB_BLOCK] + r).astype(BF16) for ls, cs, r in zip(lss_n, css_n, rs_n)]
        top = [accs[p][0:SB_NARROW]
               + jnp.dot(jnp.concatenate([a_n[p][:SB_NARROW], a_n[p][SB_NARROW:]], axis=1), v2_ref[p, j3],
                         preferred_element_type=F32) for p in pairs]
        rs_n = [r + cs[:, SB_BLOCK:] for r, cs in zip(rs_n, css_n)]
        store_state(rs, accs)
        for p in pairs:
            r_ref[p, 0:SB_NARROW, :] = rs_n[p][:SB_NARROW]
            r_ref[p, SB_BLOCK:SB_BLOCK + SB_NARROW, :] = rs_n[p][SB_NARROW:]
            acc_ref[0:SB_NARROW, cols(p)] = top[p]
        flag_ref[1] = alive_flag(rest)
        flag_ref[0] = jnp.maximum(alive_flag(rest), alive_flag(rs_n))

    @pl.when(flag_ref[1] > 0)
    def _():
        rs, accs = sweep([i - 2], [qrow >= SB_NARROW], *load_state())
        store_state(rs, accs)
        flag_ref[0] = alive_flag(rs)

    def cond(carry):
        t, alive = carry
        return jnp.logical_and(t < i, alive > 0)

    def body(carry):
        t, _ = carry
        j = i - 1 - t
        rs, accs = sweep([j], [None], *load_state())
        store_state(rs, accs)
        return t + 1, alive_flag(rs)
    lax.while_loop(cond, body, (jnp.int32(SB_LEAD - 1), flag_ref[0]))

    o_ref[...] = acc_ref[...].astype(o_ref.dtype)


def _stick_breaking(proj, qg2, kg2, cum_mat, half_ones2, batch, seq):
    nq = seq // SB_BLOCK
    npair = SB_HEADS // 2
    qcol = (A_QKV + DN_WIDTH) // SB_WIDTH
    return pl.pallas_call(
        _sb_kernel,
        grid=(batch, nq),
        in_specs=[
            pl.BlockSpec((seq, SB_WIDTH), lambda b, i: (b, qcol)),
            pl.BlockSpec((seq, SB_WIDTH), lambda b, i: (b, qcol + 1)),
            pl.BlockSpec((seq, SB_WIDTH), lambda b, i: (b, qcol + 2)),
            pl.BlockSpec((1, LANES), lambda b, i: (0, 0)),
            pl.BlockSpec((1, LANES), lambda b, i: (0, 0)),
            pl.BlockSpec((2 * SB_BLOCK, 2 * SB_BLOCK), lambda b, i: (0, 0)),
            pl.BlockSpec((2 * LANES, LANES), lambda b, i: (0, 0)),
        ],
        out_specs=pl.BlockSpec((SB_BLOCK, SB_WIDTH), lambda b, i: (b * nq + i, 0)),
        out_shape=jax.ShapeDtypeStruct((batch * seq, SB_WIDTH), BF16),
        scratch_shapes=[
            pltpu.VMEM((npair, nq, 2 * SB_BLOCK, LANES), BF16),
            pltpu.VMEM((seq, SB_WIDTH), BF16),
            pltpu.VMEM((npair, nq, 2 * SB_BLOCK, LANES), BF16),
            pltpu.VMEM((npair, 2 * SB_BLOCK, SB_BLOCK), F32),
            pltpu.VMEM((SB_BLOCK, SB_WIDTH), F32),
            pltpu.SMEM((2,), jnp.int32),
        ],
        compiler_params=pltpu.CompilerParams(
            dimension_semantics=("arbitrary", "arbitrary"), vmem_limit_bytes=VMEM_LIMIT),
        name="stickbreak",
    )(proj, proj, proj, qg2, kg2, cum_mat, half_ones2)


def _merge_kernel(x_ref, gain_ref, oa_ref, ob_ref, zb_ref, wg_ref, bg_ref, wa_ref, wb_ref, wo_ref, out_ref):
    x = x_ref[...]
    xn = (x * lax.rsqrt(jnp.mean(x * x, axis=-1, keepdims=True) + EPS) * gain_ref[...]).astype(BF16)
    gates = jax.nn.sigmoid(jnp.dot(xn, wg_ref[...], preferred_element_type=F32) + bg_ref[...])
    zb = zb_ref[...].astype(F32)
    ob = (ob_ref[...].astype(F32) * _silu(zb)).astype(BF16)
    ya = jnp.dot(oa_ref[...], wa_ref[...], preferred_element_type=F32)
    yb = jnp.dot(ob, wb_ref[...], preferred_element_type=F32)
    merged = gates[:, :D_MODEL] * ya + gates[:, D_MODEL:] * yb
    out_ref[...] = x + jnp.dot(merged.astype(BF16), wo_ref[...], preferred_element_type=F32)


def _merge(x2, gain, o_a, o_b, proj, w_gate, b_gate, w_up_a, w_up_b, w_out):
    t = x2.shape[0]
    zb_col = (A_QKV + DN_WIDTH + 3 * SB_WIDTH) // SB_WIDTH
    const = lambda i: (0, 0)
    return pl.pallas_call(
        _merge_kernel,
        grid=(t // OUT_TM,),
        in_specs=[
            pl.BlockSpec((OUT_TM, D_MODEL), lambda i: (i, 0)),
            pl.BlockSpec((1, D_MODEL), const),
            pl.BlockSpec((OUT_TM, DN_WIDTH), lambda i: (i, 0)),
            pl.BlockSpec((OUT_TM, SB_WIDTH), lambda i: (i, 0)),
            pl.BlockSpec((OUT_TM, SB_WIDTH), lambda i: (i, zb_col)),
            pl.BlockSpec((D_MODEL, 2 * D_MODEL), const),
            pl.BlockSpec((1, 2 * D_MODEL), const),
            pl.BlockSpec((DN_WIDTH, D_MODEL), const),
            pl.BlockSpec((SB_WIDTH, D_MODEL), const),
            pl.BlockSpec((D_MODEL, D_MODEL), const),
        ],
        out_specs=pl.BlockSpec((OUT_TM, D_MODEL), lambda i: (i, 0)),
        out_shape=jax.ShapeDtypeStruct((t, D_MODEL), F32),
        compiler_params=pltpu.CompilerParams(
            dimension_semantics=("arbitrary",), vmem_limit_bytes=VMEM_LIMIT),
        name="merge",
    )(x2, gain, o_a, o_b, proj, w_gate, b_gate, w_up_a, w_up_b, w_out)


def _layer(x, norm_gain, w_in, b_gate, conv_w, a_log, dt_bias, dn_out_gain,
           sb_q_gain, sb_k_gain, w_up_a, w_up_b, w_out):
    batch, seq, _ = x.shape
    x2 = x.reshape(batch * seq, D_MODEL)

    w_main = jnp.concatenate([w_in[:, :BA_COL0], w_in[:, BA_COL0 + BA_COLS:GATE_COL0]], axis=1).astype(BF16)
    w_gate = w_in[:, GATE_COL0:].astype(BF16)
    w_ba = jnp.pad(w_in[:, BA_COL0:BA_COL0 + BA_COLS], ((0, 0), (0, LANES - BA_COLS)))
    wba_hi = w_ba.astype(BF16)
    wba_lo = (w_ba - wba_hi.astype(F32)).astype(BF16)
    par = jnp.zeros((8, LANES), F32)
    par = par.at[0, DN_HEADS:2 * DN_HEADS].set(a_log).at[1, DN_HEADS:2 * DN_HEADS].set(dt_bias)
    qg2 = jnp.concatenate([sb_q_gain, sb_q_gain])[None, :]
    kg2 = jnp.concatenate([sb_k_gain, sb_k_gain])[None, :]
    kk = jnp.arange(SB_BLOCK)
    cum1 = jnp.concatenate([(kk[:, None] > kk[None, :]).astype(BF16), jnp.ones((SB_BLOCK, SB_BLOCK), BF16)], axis=1)
    cum_mat = jnp.concatenate([cum1, cum1], axis=0)
    ln = jnp.arange(LANES)
    half1 = ((ln[:, None] // SB_DH) == (ln[None, :] // SB_DH)).astype(BF16)
    half_ones2 = jnp.concatenate([half1, half1], axis=0)
    tt = jnp.arange(PAIR)
    ltri1 = ((tt[:, None] // CHUNK == tt[None, :] // CHUNK) & (tt[:, None] >= tt[None, :])).astype(BF16)
    ltri3 = jnp.concatenate([ltri1, ltri1, ltri1], axis=1)

    proj, ba = _inproj(x2, norm_gain[None, :], w_main, wba_hi, wba_lo)
    o_a = _deltanet(proj, ba, conv_w, par, dn_out_gain[None, :], ltri3, batch, seq)
    o_b = _stick_breaking(proj, qg2, kg2, cum_mat, half_ones2, batch, seq)
    out = _merge(x2, norm_gain[None, :], o_a, o_b, proj, w_gate, b_gate[None, :], w_up_a.astype(BF16),
                 w_up_b.astype(BF16), w_out.astype(BF16))
    return out.reshape(batch, seq, D_MODEL)


def kernel(x, norm_gain, w_in, b_gate, conv_w, a_log, dt_bias, dn_out_gain, sb_q_gain, sb_k_gain,
           w_up_a, w_up_b, w_out):
    h = x
    for layer in range(norm_gain.shape[0]):
        h = _layer(h, norm_gain[layer], w_in[layer], b_gate[layer], conv_w[layer], a_log[layer],
                   dt_bias[layer], dn_out_gain[layer], sb_q_gain[layer], sb_k_gain[layer],
                   w_up_a[layer], w_up_b[layer], w_out[layer])
    return h
```

```python
import functools

import jax
import jax.numpy as jnp
from jax import lax
from jax.experimental import pallas as pl
from jax.experimental.pallas import tpu as pltpu

F32 = jnp.float32
BF16 = jnp.bfloat16

D_MODEL = 1024
CHUNK = 64
SB_BLOCK = 128
EPS = 1e-6
LOG2E = 1.4426950408889634
ZERO_WEIGHT_LOG2 = -151.0
SB_LEAD = 3
SB_NARROW = 32
DN_HEADS = 4
DN_DK = 128
DN_DV = 128
DN_CONV = 4
SB_HEADS = 8
SB_DH = 64
DN_WIDTH = DN_HEADS * DN_DV
SB_WIDTH = SB_HEADS * SB_DH
A_QKV = 2 * DN_HEADS * DN_DK + DN_WIDTH
BA_COL0 = A_QKV + DN_WIDTH
BA_COLS = 2 * DN_HEADS
MAIN_WIDTH = A_QKV + DN_WIDTH + 3 * SB_WIDTH + SB_WIDTH
GATE_COL0 = BA_COL0 + BA_COLS + 4 * SB_WIDTH

LANES = 128
PAIR = 2 * CHUNK

IN_TM, IN_TN = 1024, 2048
DN_TS = 1024
DN_WAVE = 256
OUT_TM = 1024

VMEM_LIMIT = 48 * 1024 * 1024


def _split3(x):
    hi = x.astype(BF16)
    r1 = x - hi.astype(F32)
    mid = r1.astype(BF16)
    lo = (r1 - mid.astype(F32)).astype(BF16)
    return hi, mid, lo


def _split2(x):
    hi = x.astype(BF16)
    lo = (x - hi.astype(F32)).astype(BF16)
    return hi, lo


def _silu(x):
    h = 0.5 * x
    return h + h * jnp.tanh(h)


def _inproj_kernel(x_ref, gain_ref, w_ref, wba_hi_ref, wba_lo_ref, proj_ref, ba_ref, xn_ref):
    j = pl.program_id(1)

    @pl.when(j == 0)
    def _():
        half = x_ref.shape[0] // 2
        for r in range(2):
            rows = slice(r * half, (r + 1) * half)
            x = x_ref[rows, :]
            xn = x * lax.rsqrt(jnp.mean(x * x, axis=-1, keepdims=True) + EPS) * gain_ref[...]
            hi, lo = _split2(xn)
            xn_ref[rows, :] = hi
            ba_ref[rows, :] = (jnp.dot(hi, wba_hi_ref[...], preferred_element_type=F32)
                               + jnp.dot(lo, wba_hi_ref[...], preferred_element_type=F32)
                               + jnp.dot(hi, wba_lo_ref[...], preferred_element_type=F32))
            proj_ref[rows, :] = jnp.dot(hi, w_ref[...], preferred_element_type=F32).astype(proj_ref.dtype)

    @pl.when(j != 0)
    def _():
        proj_ref[...] = jnp.dot(xn_ref[...], w_ref[...], preferred_element_type=F32).astype(proj_ref.dtype)


def _inproj(x2, gain, w_main, wba_hi, wba_lo):
    t = x2.shape[0]
    return pl.pallas_call(
        _inproj_kernel,
        grid=(t // IN_TM, MAIN_WIDTH // IN_TN),
        in_specs=[
            pl.BlockSpec((IN_TM, D_MODEL), lambda i, j: (i, 0)),
            pl.BlockSpec((1, D_MODEL), lambda i, j: (0, 0)),
            pl.BlockSpec((D_MODEL, IN_TN), lambda i, j: (0, j)),
            pl.BlockSpec((D_MODEL, LANES), lambda i, j: (0, 0)),
            pl.BlockSpec((D_MODEL, LANES), lambda i, j: (0, 0)),
        ],
        out_specs=[
            pl.BlockSpec((IN_TM, IN_TN), lambda i, j: (i, j)),
            pl.BlockSpec((IN_TM, LANES), lambda i, j: (i, 0)),
        ],
        out_shape=[
            jax.ShapeDtypeStruct((t, MAIN_WIDTH), BF16),
            jax.ShapeDtypeStruct((t, LANES), F32),
        ],
        scratch_shapes=[pltpu.VMEM((IN_TM, D_MODEL), BF16)],
        compiler_params=pltpu.CompilerParams(
            dimension_semantics=("arbitrary", "arbitrary"), vmem_limit_bytes=VMEM_LIMIT),
        name="inproj",
    )(x2, gain, w_main, wba_hi, wba_lo)


def _tri_inverse(ms, row, col):
    blk8 = (row >> 3) == (col >> 3)
    eye = jnp.where(row == col, 1.0, 0.0)
    pds = [jnp.where(blk8, -m, 0.0) for m in ms]
    pdb = [pd.astype(BF16) for pd in pds]
    xs = [eye + pd for pd in pds]
    p2b = [jnp.dot(b, b, preferred_element_type=F32).astype(BF16) for b in pdb]
    yield
    xs = [x + jnp.dot(x.astype(BF16), p2, preferred_element_type=F32) for x, p2 in zip(xs, p2b)]
    yield
    p4b = [jnp.dot(b, b, preferred_element_type=F32).astype(BF16) for b in p2b]
    yield
    xs = [x + jnp.dot(x.astype(BF16), p4, preferred_element_type=F32) for x, p4 in zip(xs, p4b)]
    yield
    for s in (8, 16, 32):
        sh = s.bit_length()
        off = ((row >> sh) == (col >> sh)) & ((row & (2 * s - 1)) >= s) & ((col & (2 * s - 1)) < s)
        xb = [x.astype(BF16) for x in xs]
        ts_ = [jnp.dot(b, jnp.where(off, m, 0.0).astype(BF16), preferred_element_type=F32).astype(BF16)
               for b, m in zip(xb, ms)]
        yield
        xs = [x - jnp.dot(t, b, preferred_element_type=F32) for x, t, b in zip(xs, ts_, xb)]
        yield
    return xs


def _round_robin(*gens):
    live = list(gens)
    while live:
        for g in list(live):
            try:
                next(g)
            except StopIteration:
                live.remove(g)


def _deltanet_kernel(qkv_ref, z_ref, ba_ref, convw_ref, par_ref, gain_ref, ltri_ref, o_ref,
                     buf_ref, act_ref, state_ref, ol_ref, qt_ref, mc_ref, bc_ref, ge_ref):
    ts = qkv_ref.shape[0]
    nqk = DN_HEADS * DN_DK
    heads = range(DN_HEADS)
    nslab = A_QKV // LANES
    nwave = ts // DN_WAVE

    @pl.when(pl.program_id(1) == 0)
    def _():
        buf_ref[:, 0:8, :] = jnp.zeros((nslab, 8, LANES), F32)
        state_ref[...] = jnp.zeros_like(state_ref)

    for s in range(nslab):
        buf_ref[s, 8:8 + ts, :] = qkv_ref[:, s * LANES:(s + 1) * LANES].astype(F32)

    def activations(w):
        for r0 in range(w * DN_WAVE, (w + 1) * DN_WAVE, PAIR):
            for s in range(nslab):
                cs = slice(s * LANES, (s + 1) * LANES)
                acc = None
                for i in range(DN_CONV):
                    lo = 8 - (DN_CONV - 1) + i + r0
                    term = buf_ref[s, lo:lo + PAIR, :] * convw_ref[i:i + 1, cs]
                    acc = term if acc is None else acc + term
                y = _silu(acc)
                if s < 2 * DN_HEADS:
                    y = y * lax.rsqrt(jnp.sum(y * y, axis=-1, keepdims=True) + EPS)
                if s < DN_HEADS:
                    y = y * (DN_DK ** -0.5)
                act_ref[r0:r0 + PAIR, cs] = y
                yield

    ba = ba_ref[...]
    beta_all = jax.nn.sigmoid(ba)
    xs = ba + par_ref[1:2, :]
    softplus = jnp.maximum(xs, 0.0) + jnp.log(1.0 + jnp.exp(-jnp.abs(xs)))
    g_all = -jnp.exp(par_ref[0:1, :]) * softplus
    gc_of = [jnp.dot(ltri_ref[...], jnp.concatenate(_split3(g_all[p * PAIR:(p + 1) * PAIR, :]), axis=0),
                     preferred_element_type=F32) for p in range(ts // PAIR)]

    row = lax.broadcasted_iota(jnp.int32, (PAIR, PAIR), 0)
    col = lax.broadcasted_iota(jnp.int32, (PAIR, PAIR), 1)
    same = (row >> 6) == (col >> 6)
    lower = same & (row >= col)
    strict = same & (row > col)
    rcol = lax.broadcasted_iota(jnp.int32, (PAIR, 1), 0)

    def chunk_local(w):
        pairs = range(w * (DN_WAVE // PAIR), (w + 1) * (DN_WAVE // PAIR))
        units = [(p, h) for p in pairs for h in heads]
        rows_of = [slice(p * PAIR, (p + 1) * PAIR) for p, _ in units]
        gc_t = {p: gc_of[p].T for p in pairs}
        q = [act_ref[rs, h * DN_DK:(h + 1) * DN_DK] for rs, (_, h) in zip(rows_of, units)]
        k = [act_ref[rs, nqk + h * DN_DK:nqk + (h + 1) * DN_DK] for rs, (_, h) in zip(rows_of, units)]
        v = [act_ref[rs, 2 * nqk + h * DN_DV:2 * nqk + (h + 1) * DN_DV] for rs, (_, h) in zip(rows_of, units)]
        beta = [beta_all[rs, h:h + 1] for rs, (_, h) in zip(rows_of, units)]
        gcc = [gc_of[p][:, DN_HEADS + h:DN_HEADS + h + 1] for p, h in units]
        decay = [jnp.where(lower, jnp.exp(g - gc_t[p][DN_HEADS + h:DN_HEADS + h + 1, :]), 0.0)
                 for g, (p, h) in zip(gcc, units)]
        eg = [jnp.exp(g) for g in gcc]
        gend = [jnp.where(rcol < CHUNK, g[CHUNK - 1:CHUNK, :], g[PAIR - 1:PAIR, :]) for g in gcc]
        kb = [x.astype(BF16) for x in k]
        yield
        qkk = [lax.dot_general(jnp.concatenate([qq.astype(BF16), kk], axis=0), kk,
                               (((1,), (1,)), ((), ())), preferred_element_type=F32) for qq, kk in zip(q, kb)]
        yield
        ms = [jnp.where(strict, b * x[PAIR:] * d, 0.0) for b, x, d in zip(beta, qkk, decay)]
        tinv = yield from _tri_inverse(ms, row, col)
        sol = [jnp.dot(t.astype(BF16),
                       jnp.concatenate([(b * vv).astype(BF16), (b * e * kk).astype(BF16)], axis=1),
                       preferred_element_type=F32) for t, b, vv, e, kk in zip(tinv, beta, v, eg, k)]
        yield
        solb = [x.astype(BF16) for x in sol]
        qkb = [jnp.where(lower, x[:PAIR] * d, 0.0).astype(BF16) for x, d in zip(qkk, decay)]
        qwu = [jnp.dot(a, b, preferred_element_type=F32) for a, b in zip(qkb, solb)]
        yield
        kd = [kk * jnp.exp(ge - g) for kk, ge, g in zip(k, gend, gcc)]
        for c in range(PAIR // CHUNK):
            kdc = [jnp.where((row >> 6) == c, x, 0.0).astype(BF16) for x in kd]
            kwu = [lax.dot_general(a, b, (((0,), (0,)), ((), ())), preferred_element_type=F32)
                   for a, b in zip(kdc, solb)]
            yield
            for (p, h), x, g in zip(units, kwu, gcc):
                cc = p * (PAIR // CHUNK) + c
                bc_ref[h, cc] = x[:, :DN_DV]
                mc_ref[h, cc] = x[:, DN_DV:].astype(BF16)
                ge_ref[h, cc] = jnp.broadcast_to(jnp.exp(g[(c + 1) * CHUNK - 1:(c + 1) * CHUNK, :]), (8, LANES))
        for (p, h), rs, x, qq, e in zip(units, rows_of, qwu, q, eg):
            ol_ref[h, rs, :] = x[:, :DN_DV]
            qt_ref[h, rs, :] = (qq * e - x[:, DN_DV:]).astype(BF16)
        yield

    gain = gain_ref[...]
    st = [state_ref[h] for h in heads]

    def recurrence(w):
        for cc in range(w * (DN_WAVE // CHUNK), (w + 1) * (DN_WAVE // CHUNK)):
            cr = slice(cc * CHUNK, (cc + 1) * CHUNK)
            sb = [x.astype(BF16) for x in st]
            o = [ol_ref[h, cr, :] + jnp.dot(qt_ref[h, cr, :], sb[h], preferred_element_type=F32) for h in heads]
            new = [st[h] * ge_ref[h, cc, 0:1, :] + bc_ref[h, cc]
                   - jnp.dot(mc_ref[h, cc], sb[h], preferred_element_type=F32) for h in heads]
            for h in heads:
                st[h] = new[h]
            yield
            for h in heads:
                hs = slice(h * DN_DV, (h + 1) * DN_DV)
                on = o[h] * lax.rsqrt(jnp.mean(o[h] * o[h], axis=-1, keepdims=True) + EPS) * gain
                zz = z_ref[cr, hs].astype(F32)
                o_ref[cr, hs] = (on * _silu(zz)).astype(o_ref.dtype)
            yield

    _round_robin(activations(0))
    for w in range(nwave):
        others = []
        if w + 1 < nwave:
            others.append(activations(w + 1))
        if w > 0:
            others.append(recurrence(w - 1))
        _round_robin(chunk_local(w), *others)
    _round_robin(recurrence(nwave - 1))
    for h in heads:
        state_ref[h] = st[h]
    for s in range(nslab):
        buf_ref[s, 0:8, :] = buf_ref[s, ts:ts + 8, :]


def _deltanet(proj, ba, conv_w, par, gain, ltri3, batch, seq):
    nsb = seq // DN_TS
    nchunk = DN_TS // CHUNK
    return pl.pallas_call(
        _deltanet_kernel,
        grid=(batch, nsb),
        in_specs=[
            pl.BlockSpec((DN_TS, A_QKV), lambda b, s: (b * nsb + s, 0)),
            pl.BlockSpec((DN_TS, DN_WIDTH), lambda b, s: (b * nsb + s, A_QKV // DN_WIDTH)),
            pl.BlockSpec((DN_TS, LANES), lambda b, s: (b * nsb + s, 0)),
            pl.BlockSpec((DN_CONV, A_QKV), lambda b, s: (0, 0)),
            pl.BlockSpec((8, LANES), lambda b, s: (0, 0)),
            pl.BlockSpec((1, DN_DV), lambda b, s: (0, 0)),
            pl.BlockSpec((PAIR, 3 * PAIR), lambda b, s: (0, 0)),
        ],
        out_specs=pl.BlockSpec((DN_TS, DN_WIDTH), lambda b, s: (b * nsb + s, 0)),
        out_shape=jax.ShapeDtypeStruct((batch * seq, DN_WIDTH), BF16),
        scratch_shapes=[
            pltpu.VMEM((A_QKV // LANES, DN_TS + 8, LANES), F32),
            pltpu.VMEM((DN_TS, A_QKV), F32),
            pltpu.VMEM((DN_HEADS, DN_DK, DN_DV), F32),
            pltpu.VMEM((DN_HEADS, DN_TS, DN_DV), F32),
            pltpu.VMEM((DN_HEADS, DN_TS, DN_DK), BF16),
            pltpu.VMEM((DN_HEADS, nchunk, DN_DK, DN_DK), BF16),
            pltpu.VMEM((DN_HEADS, nchunk, DN_DK, DN_DV), F32),
            pltpu.VMEM((DN_HEADS, nchunk, 8, LANES), F32),
        ],
        compiler_params=pltpu.CompilerParams(
            dimension_semantics=("arbitrary", "arbitrary"), vmem_limit_bytes=VMEM_LIMIT),
        name="deltanet",
    )(proj, proj, ba, conv_w, par, gain, ltri3)


def _rms_halves(x, half_ones2, gain):
    hi, lo = _split2(x * x)
    ss = jnp.dot(jnp.concatenate([hi, lo], axis=1), half_ones2, preferred_element_type=F32)
    return x * lax.rsqrt(ss + SB_DH * EPS) * (gain * SB_DH ** 0.5)


def _sb_kernel(q_ref, k_ref, v_ref, qg_ref, kg_ref, cum_ref, half_ref, o_ref,
               q2_ref, kn_ref, v2_ref, r_ref, acc_ref, flag_ref):
    i = pl.program_id(1)
    nblk = k_ref.shape[0] // SB_BLOCK
    npair = k_ref.shape[1] // LANES
    pairs = range(npair)
    lane = lax.broadcasted_iota(jnp.int32, (SB_BLOCK, LANES), 1)
    first = lane < SB_DH

    def cols(p):
        return slice(p * LANES, (p + 1) * LANES)

    @pl.when(i == 0)
    def _():
        qscale = qg_ref[...] * (SB_DH ** -0.5 * LOG2E)

        def prep(j, carry):
            rows = pl.ds(pl.multiple_of(j * SB_BLOCK, SB_BLOCK), SB_BLOCK)
            for p in pairs:
                qn = _rms_halves(q_ref[rows, cols(p)].astype(F32), half_ref[...], qscale).astype(BF16)
                zero = jnp.zeros_like(qn)
                q2_ref[p, j, 0:SB_BLOCK, :] = jnp.where(first, qn, zero)
                q2_ref[p, j, SB_BLOCK:2 * SB_BLOCK, :] = jnp.where(first, zero, qn)
                kn = _rms_halves(k_ref[rows, cols(p)].astype(F32), half_ref[...], kg_ref[...])
                kn_ref[rows, cols(p)] = kn.astype(BF16)
                vb = v_ref[rows, cols(p)]
                v2_ref[p, j, 0:SB_BLOCK, :] = jnp.where(first, vb, zero)
                v2_ref[p, j, SB_BLOCK:2 * SB_BLOCK, :] = jnp.where(first, zero, vb)
            return carry
        lax.fori_loop(0, nblk, prep, 0)

    q2 = [q2_ref[p, i] for p in pairs]

    def scores(rows, mask):
        zs = [lax.dot_general(q2[p], kn_ref[rows, cols(p)], (((1,), (1,)), ((), ())),
                              preferred_element_type=F32) for p in pairs]
        lss = [jnp.minimum(z, 0.0) - jnp.log(1.0 + jnp.exp2(-jnp.abs(z))) * LOG2E for z in zs]
        lks = [ls - z for ls, z in zip(lss, zs)]
        if mask is not None:
            lks = [jnp.where(mask, lk, 0.0) for lk in lks]
        return lss, [jnp.concatenate(_split2(lk), axis=1) for lk in lks]

    def cumsums(lk2s):
        return [jnp.dot(lk2, cum_ref[...], preferred_element_type=F32) for lk2 in lk2s]

    def weights(lss, css, rs, mask):
        if rs is None:
            avs = [jnp.exp2(ls + cs[:, :SB_BLOCK]) for ls, cs in zip(lss, css)]
        else:
            avs = [jnp.exp2(ls + cs[:, :SB_BLOCK] + r) for ls, cs, r in zip(lss, css, rs)]
        if mask is not None:
            avs = [jnp.where(mask, a, 0.0) for a in avs]
        out = []
        for a in avs:
            ab = a.astype(BF16)
            out.append(jnp.concatenate([ab[:SB_BLOCK], ab[SB_BLOCK:]], axis=1))
        return out

    def tile_rows(j):
        return pl.ds(pl.multiple_of(j * SB_BLOCK, SB_BLOCK), SB_BLOCK)

    def sweep(tiles, masks, rs, accs):
        sc = [scores(tile_rows(j), m) for j, m in zip(tiles, masks)]
        css = [cumsums(lk2s) for _, lk2s in sc]
        for j, m, (lss, _), cs in zip(tiles, masks, sc, css):
            a2s = weights(lss, cs, rs, m)
            rs = [c[:, SB_BLOCK:] for c in cs] if rs is None else [r + c[:, SB_BLOCK:] for r, c in zip(rs, cs)]
            accs = [acc + jnp.dot(a2, v2_ref[p, j], preferred_element_type=F32)
                    for p, (acc, a2) in enumerate(zip(accs, a2s))]
        return rs, accs

    def load_state():
        return [r_ref[p] for p in pairs], [acc_ref[:, cols(p)] for p in pairs]

    def store_state(rs, accs):
        for p in pairs:
            r_ref[p] = rs[p]
            acc_ref[:, cols(p)] = accs[p]

    qrow = lax.broadcasted_iota(jnp.int32, (2 * SB_BLOCK, SB_BLOCK), 0) & (SB_BLOCK - 1)
    kcol = lax.broadcasted_iota(jnp.int32, (2 * SB_BLOCK, SB_BLOCK), 1)
    causal = kcol < qrow
    zero_accs = [jnp.zeros((SB_BLOCK, LANES), F32)] * npair

    def alive_flag(rs):
        return (jnp.max(functools.reduce(jnp.maximum, rs)) >= ZERO_WEIGHT_LOG2).astype(jnp.int32)

    def head_rows(x, lo, hi):
        return jnp.concatenate([x[lo:hi], x[SB_BLOCK + lo:SB_BLOCK + hi]], axis=0)

    @pl.when(i == 0)
    def _():
        rs, accs = sweep([i], [causal], None, zero_accs)
        store_state(rs, accs)
        flag_ref[0] = alive_flag(rs)
        flag_ref[1] = jnp.int32(0)

    @pl.when(i == 1)
    def _():
        rs, accs = sweep([i, i - 1], [causal, None], None, zero_accs)
        store_state(rs, accs)
        flag_ref[0] = alive_flag(rs)
        flag_ref[1] = jnp.int32(0)

    @pl.when(i >= 2)
    def _():
        j3 = i - 2
        sc = [scores(tile_rows(j), m) for j, m in ((i, causal), (i - 1, None))]
        q2n = [head_rows(q, 0, SB_NARROW) for q in q2]
        zs = [lax.dot_general(q2n[p], kn_ref[tile_rows(j3), cols(p)], (((1,), (1,)), ((), ())),
                              preferred_element_type=F32) for p in pairs]
        lss_n = [jnp.minimum(z, 0.0) - jnp.log(1.0 + jnp.exp2(-jnp.abs(z))) * LOG2E for z in zs]
        lk2_n = [jnp.concatenate(_split2(ls - z), axis=1) for ls, z in zip(lss_n, zs)]
        css = [cumsums(lk2s) for _, lk2s in sc]
        css_n = cumsums(lk2_n)
        rs, accs = None, zero_accs
        for j, m, (lss, _), cs in zip((i, i - 1), (causal, None), sc, css):
            a2s = weights(lss, cs, rs, m)
            rs = [c[:, SB_BLOCK:] for c in cs] if rs is None else [r + c[:, SB_BLOCK:] for r, c in zip(rs, cs)]
            accs = [acc + jnp.dot(a2, v2_ref[p, j], preferred_element_type=F32)
                    for p, (acc, a2) in enumerate(zip(accs, a2s))]
        rest = [head_rows(r, SB_NARROW, SB_BLOCK) for r in rs]
        rs_n = [head_rows(r, 0, SB_NARROW) for r in rs]
        a_n = [jnp.exp2(ls + cs[:, :SB_BLOCK] + r).astype(BF16) for ls, cs, r in zip(lss_n, css_n, rs_n)]
        top = [accs[p][0:SB_NARROW]
               + jnp.dot(jnp.concatenate([a_n[p][:SB_NARROW], a_n[p][SB_NARROW:]], axis=1), v2_ref[p, j3],
                         preferred_element_type=F32) for p in pairs]
        rs_n = [r + cs[:, SB_BLOCK:] for r, cs in zip(rs_n, css_n)]
        store_state(rs, accs)
        for p in pairs:
            r_ref[p, 0:SB_NARROW, :] = rs_n[p][:SB_NARROW]
            r_ref[p, SB_BLOCK:SB_BLOCK + SB_NARROW, :] = rs_n[p][SB_NARROW:]
            acc_ref[0:SB_NARROW, cols(p)] = top[p]
        flag_ref[1] = alive_flag(rest)
        flag_ref[0] = jnp.maximum(alive_flag(rest), alive_flag(rs_n))

    @pl.when(flag_ref[1] > 0)
    def _():
        rs, accs = sweep([i - 2], [qrow >= SB_NARROW], *load_state())
        store_state(rs, accs)
        flag_ref[0] = alive_flag(rs)

    def cond(carry):
        t, alive = carry
        return jnp.logical_and(t < i, alive > 0)

    def body(carry):
        t, _ = carry
        j = i - 1 - t
        rs, accs = sweep([j], [None], *load_state())
        store_state(rs, accs)
        return t + 1, alive_flag(rs)
    lax.while_loop(cond, body, (jnp.int32(SB_LEAD - 1), flag_ref[0]))

    o_ref[...] = acc_ref[...].astype(o_ref.dtype)


def _stick_breaking(proj, qg2, kg2, cum_mat, half_ones2, batch, seq):
    nq = seq // SB_BLOCK
    npair = SB_HEADS // 2
    qcol = (A_QKV + DN_WIDTH) // SB_WIDTH
    return pl.pallas_call(
        _sb_kernel,
        grid=(batch, nq),
        in_specs=[
            pl.BlockSpec((seq, SB_WIDTH), lambda b, i: (b, qcol)),
            pl.BlockSpec((seq, SB_WIDTH), lambda b, i: (b, qcol + 1)),
            pl.BlockSpec((seq, SB_WIDTH), lambda b, i: (b, qcol + 2)),
            pl.BlockSpec((1, LANES), lambda b, i: (0, 0)),
            pl.BlockSpec((1, LANES), lambda b, i: (0, 0)),
            pl.BlockSpec((2 * SB_BLOCK, 2 * SB_BLOCK), lambda b, i: (0, 0)),
            pl.BlockSpec((2 * LANES, LANES), lambda b, i: (0, 0)),
        ],
        out_specs=pl.BlockSpec((SB_BLOCK, SB_WIDTH), lambda b, i: (b * nq + i, 0)),
        out_shape=jax.ShapeDtypeStruct((batch * seq, SB_WIDTH), BF16),
        scratch_shapes=[
            pltpu.VMEM((npair, nq, 2 * SB_BLOCK, LANES), BF16),
            pltpu.VMEM((seq, SB_WIDTH), BF16),
            pltpu.VMEM((npair, nq, 2 * SB_BLOCK, LANES), BF16),
            pltpu.VMEM((npair, 2 * SB_BLOCK, SB_BLOCK), F32),
            pltpu.VMEM((SB_BLOCK, SB_WIDTH), F32),
            pltpu.SMEM((2,), jnp.int32),
        ],
        compiler_params=pltpu.CompilerParams(
            dimension_semantics=("arbitrary", "arbitrary"), vmem_limit_bytes=VMEM_LIMIT),
        name="stickbreak",
    )(proj, proj, proj, qg2, kg2, cum_mat, half_ones2)


def _merge_kernel(x_ref, gain_ref, oa_ref, ob_ref, zb_ref, wg_ref, bg_ref, wa_ref, wb_ref, wo_ref, out_ref):
    x = x_ref[...]
    xn = (x * lax.rsqrt(jnp.mean(x * x, axis=-1, keepdims=True) + EPS) * gain_ref[...]).astype(BF16)
    gates = jax.nn.sigmoid(jnp.dot(xn, wg_ref[...], preferred_element_type=F32) + bg_ref[...])
    zb = zb_ref[...].astype(F32)
    ob = (ob_ref[...].astype(F32) * _silu(zb)).astype(BF16)
    ya = jnp.dot(oa_ref[...], wa_ref[...], preferred_element_type=F32)
    yb = jnp.dot(ob, wb_ref[...], preferred_element_type=F32)
    merged = gates[:, :D_MODEL] * ya + gates[:, D_MODEL:] * yb
    out_ref[...] = x + jnp.dot(merged.astype(BF16), wo_ref[...], preferred_element_type=F32)


def _merge(x2, gain, o_a, o_b, proj, w_gate, b_gate, w_up_a, w_up_b, w_out):
    t = x2.shape[0]
    zb_col = (A_QKV + DN_WIDTH + 3 * SB_WIDTH) // SB_WIDTH
    const = lambda i: (0, 0)
    return pl.pallas_call(
        _merge_kernel,
        grid=(t // OUT_TM,),
        in_specs=[
            pl.BlockSpec((OUT_TM, D_MODEL), lambda i: (i, 0)),
            pl.BlockSpec((1, D_MODEL), const),
            pl.BlockSpec((OUT_TM, DN_WIDTH), lambda i: (i, 0)),
            pl.BlockSpec((OUT_TM, SB_WIDTH), lambda i: (i, 0)),
            pl.BlockSpec((OUT_TM, SB_WIDTH), lambda i: (i, zb_col)),
            pl.BlockSpec((D_MODEL, 2 * D_MODEL), const, pipeline_mode=pl.Buffered(1)),
            pl.BlockSpec((1, 2 * D_MODEL), const),
            pl.BlockSpec((DN_WIDTH, D_MODEL), const, pipeline_mode=pl.Buffered(1)),
            pl.BlockSpec((SB_WIDTH, D_MODEL), const, pipeline_mode=pl.Buffered(1)),
            pl.BlockSpec((D_MODEL, D_MODEL), const, pipeline_mode=pl.Buffered(1)),
        ],
        out_specs=pl.BlockSpec((OUT_TM, D_MODEL), lambda i: (i, 0)),
        out_shape=jax.ShapeDtypeStruct((t, D_MODEL), F32),
        compiler_params=pltpu.CompilerParams(
            dimension_semantics=("arbitrary",), vmem_limit_bytes=VMEM_LIMIT),
        name="merge",
    )(x2, gain, o_a, o_b, proj, w_gate, b_gate, w_up_a, w_up_b, w_out)


def _layer(x, norm_gain, w_in, b_gate, conv_w, a_log, dt_bias, dn_out_gain,
           sb_q_gain, sb_k_gain, w_up_a, w_up_b, w_out):
    batch, seq, _ = x.shape
    x2 = x.reshape(batch * seq, D_MODEL)

    w_main = jnp.concatenate([w_in[:, :BA_COL0], w_in[:, BA_COL0 + BA_COLS:GATE_COL0]], axis=1).astype(BF16)
    w_gate = w_in[:, GATE_COL0:].astype(BF16)
    w_ba = jnp.pad(w_in[:, BA_COL0:BA_COL0 + BA_COLS], ((0, 0), (0, LANES - BA_COLS)))
    wba_hi = w_ba.astype(BF16)
    wba_lo = (w_ba - wba_hi.astype(F32)).astype(BF16)
    par = jnp.zeros((8, LANES), F32)
    par = par.at[0, DN_HEADS:2 * DN_HEADS].set(a_log).at[1, DN_HEADS:2 * DN_HEADS].set(dt_bias)
    qg2 = jnp.concatenate([sb_q_gain, sb_q_gain])[None, :]
    kg2 = jnp.concatenate([sb_k_gain, sb_k_gain])[None, :]
    kk = jnp.arange(SB_BLOCK)
    cum1 = jnp.concatenate([(kk[:, None] > kk[None, :]).astype(BF16), jnp.ones((SB_BLOCK, SB_BLOCK), BF16)], axis=1)
    cum_mat = jnp.concatenate([cum1, cum1], axis=0)
    ln = jnp.arange(LANES)
    half1 = ((ln[:, None] // SB_DH) == (ln[None, :] // SB_DH)).astype(BF16)
    half_ones2 = jnp.concatenate([half1, half1], axis=0)
    tt = jnp.arange(PAIR)
    ltri1 = ((tt[:, None] // CHUNK == tt[None, :] // CHUNK) & (tt[:, None] >= tt[None, :])).astype(BF16)
    ltri3 = jnp.concatenate([ltri1, ltri1, ltri1], axis=1)

    proj, ba = _inproj(x2, norm_gain[None, :], w_main, wba_hi, wba_lo)
    o_a = _deltanet(proj, ba, conv_w, par, dn_out_gain[None, :], ltri3, batch, seq)
    o_b = _stick_breaking(proj, qg2, kg2, cum_mat, half_ones2, batch, seq)
    out = _merge(x2, norm_gain[None, :], o_a, o_b, proj, w_gate, b_gate[None, :], w_up_a.astype(BF16),
                 w_up_b.astype(BF16), w_out.astype(BF16))
    return out.reshape(batch, seq, D_MODEL)


def kernel(x, norm_gain, w_in, b_gate, conv_w, a_log, dt_bias, dn_out_gain, sb_q_gain, sb_k_gain,
           w_up_a, w_up_b, w_out):
    h = x
    for layer in range(norm_gain.shape[0]):
        h = _layer(h, norm_gain[layer], w_in[layer], b_gate[layer], conv_w[layer], a_log[layer],
                   dt_bias[layer], dn_out_gain[layer], sb_q_gain[layer], sb_k_gain[layer],
                   w_up_a[layer], w_up_b[layer], w_out[layer])
    return h
```

```python
import functools

import jax
import jax.numpy as jnp
from jax import lax
from jax.experimental import pallas as pl
from jax.experimental.pallas import tpu as pltpu

F32 = jnp.float32
BF16 = jnp.bfloat16

D_MODEL = 1024
CHUNK = 64
SB_BLOCK = 128
EPS = 1e-6
LOG2E = 1.4426950408889634
ZERO_WEIGHT_LOG2 = -151.0
SB_LEAD = 3
SB_QB = 4
SB_NARROW = 32
DN_HEADS = 4
DN_DK = 128
DN_DV = 128
DN_CONV = 4
SB_HEADS = 8
SB_DH = 64
DN_WIDTH = DN_HEADS * DN_DV
SB_WIDTH = SB_HEADS * SB_DH
A_QKV = 2 * DN_HEADS * DN_DK + DN_WIDTH
BA_COL0 = A_QKV + DN_WIDTH
BA_COLS = 2 * DN_HEADS
MAIN_WIDTH = A_QKV + DN_WIDTH + 3 * SB_WIDTH + SB_WIDTH
GATE_COL0 = BA_COL0 + BA_COLS + 4 * SB_WIDTH

LANES = 128
PAIR = 2 * CHUNK

IN_TM, IN_TN = 1024, 2048
DN_TS = 1024
DN_WAVE = 256
OUT_TM = 1024

VMEM_LIMIT = 48 * 1024 * 1024


def _split3(x):
    hi = x.astype(BF16)
    r1 = x - hi.astype(F32)
    mid = r1.astype(BF16)
    lo = (r1 - mid.astype(F32)).astype(BF16)
    return hi, mid, lo


def _split2(x):
    hi = x.astype(BF16)
    lo = (x - hi.astype(F32)).astype(BF16)
    return hi, lo


def _silu(x):
    h = 0.5 * x
    return h + h * jnp.tanh(h)


def _inproj_kernel(x_ref, gain_ref, w_ref, wba_hi_ref, wba_lo_ref, proj_ref, ba_ref, xn_ref):
    j = pl.program_id(1)

    @pl.when(j == 0)
    def _():
        half = x_ref.shape[0] // 2
        for r in range(2):
            rows = slice(r * half, (r + 1) * half)
            x = x_ref[rows, :]
            xn = x * lax.rsqrt(jnp.mean(x * x, axis=-1, keepdims=True) + EPS) * gain_ref[...]
            hi, lo = _split2(xn)
            xn_ref[rows, :] = hi
            ba_ref[rows, :] = (jnp.dot(hi, wba_hi_ref[...], preferred_element_type=F32)
                               + jnp.dot(lo, wba_hi_ref[...], preferred_element_type=F32)
                               + jnp.dot(hi, wba_lo_ref[...], preferred_element_type=F32))
            proj_ref[rows, :] = jnp.dot(hi, w_ref[...], preferred_element_type=F32).astype(proj_ref.dtype)

    @pl.when(j != 0)
    def _():
        proj_ref[...] = jnp.dot(xn_ref[...], w_ref[...], preferred_element_type=F32).astype(proj_ref.dtype)


def _inproj(x2, gain, w_main, wba_hi, wba_lo):
    t = x2.shape[0]
    return pl.pallas_call(
        _inproj_kernel,
        grid=(t // IN_TM, MAIN_WIDTH // IN_TN),
        in_specs=[
            pl.BlockSpec((IN_TM, D_MODEL), lambda i, j: (i, 0)),
            pl.BlockSpec((1, D_MODEL), lambda i, j: (0, 0)),
            pl.BlockSpec((D_MODEL, IN_TN), lambda i, j: (0, j)),
            pl.BlockSpec((D_MODEL, LANES), lambda i, j: (0, 0)),
            pl.BlockSpec((D_MODEL, LANES), lambda i, j: (0, 0)),
        ],
        out_specs=[
            pl.BlockSpec((IN_TM, IN_TN), lambda i, j: (i, j)),
            pl.BlockSpec((IN_TM, LANES), lambda i, j: (i, 0)),
        ],
        out_shape=[
            jax.ShapeDtypeStruct((t, MAIN_WIDTH), BF16),
            jax.ShapeDtypeStruct((t, LANES), F32),
        ],
        scratch_shapes=[pltpu.VMEM((IN_TM, D_MODEL), BF16)],
        compiler_params=pltpu.CompilerParams(
            dimension_semantics=("arbitrary", "arbitrary"), vmem_limit_bytes=VMEM_LIMIT),
        name="inproj",
    )(x2, gain, w_main, wba_hi, wba_lo)


def _tri_inverse(ms, row, col):
    blk8 = (row >> 3) == (col >> 3)
    eye = jnp.where(row == col, 1.0, 0.0)
    pds = [jnp.where(blk8, -m, 0.0) for m in ms]
    pdb = [pd.astype(BF16) for pd in pds]
    xs = [eye + pd for pd in pds]
    p2b = [jnp.dot(b, b, preferred_element_type=F32).astype(BF16) for b in pdb]
    yield
    xs = [x + jnp.dot(x.astype(BF16), p2, preferred_element_type=F32) for x, p2 in zip(xs, p2b)]
    yield
    p4b = [jnp.dot(b, b, preferred_element_type=F32).astype(BF16) for b in p2b]
    yield
    xs = [x + jnp.dot(x.astype(BF16), p4, preferred_element_type=F32) for x, p4 in zip(xs, p4b)]
    yield
    for s in (8, 16, 32):
        sh = s.bit_length()
        off = ((row >> sh) == (col >> sh)) & ((row & (2 * s - 1)) >= s) & ((col & (2 * s - 1)) < s)
        xb = [x.astype(BF16) for x in xs]
        ts_ = [jnp.dot(b, jnp.where(off, m, 0.0).astype(BF16), preferred_element_type=F32).astype(BF16)
               for b, m in zip(xb, ms)]
        yield
        xs = [x - jnp.dot(t, b, preferred_element_type=F32) for x, t, b in zip(xs, ts_, xb)]
        yield
    return xs


def _round_robin(*gens):
    live = list(gens)
    while live:
        for g in list(live):
            try:
                next(g)
            except StopIteration:
                live.remove(g)


def _deltanet_kernel(qkv_ref, z_ref, ba_ref, convw_ref, par_ref, gain_ref, ltri_ref, o_ref,
                     buf_ref, act_ref, state_ref, ol_ref, qt_ref, mc_ref, bc_ref, ge_ref):
    ts = qkv_ref.shape[0]
    nqk = DN_HEADS * DN_DK
    heads = range(DN_HEADS)
    nslab = A_QKV // LANES
    nwave = ts // DN_WAVE

    @pl.when(pl.program_id(1) == 0)
    def _():
        buf_ref[:, 0:8, :] = jnp.zeros((nslab, 8, LANES), F32)
        state_ref[...] = jnp.zeros_like(state_ref)

    for s in range(nslab):
        buf_ref[s, 8:8 + ts, :] = qkv_ref[:, s * LANES:(s + 1) * LANES].astype(F32)

    def activations(w):
        for r0 in range(w * DN_WAVE, (w + 1) * DN_WAVE, PAIR):
            for s in range(nslab):
                cs = slice(s * LANES, (s + 1) * LANES)
                acc = None
                for i in range(DN_CONV):
                    lo = 8 - (DN_CONV - 1) + i + r0
                    term = buf_ref[s, lo:lo + PAIR, :] * convw_ref[i:i + 1, cs]
                    acc = term if acc is None else acc + term
                y = _silu(acc)
                if s < 2 * DN_HEADS:
                    y = y * lax.rsqrt(jnp.sum(y * y, axis=-1, keepdims=True) + EPS)
                if s < DN_HEADS:
                    y = y * (DN_DK ** -0.5)
                act_ref[r0:r0 + PAIR, cs] = y
                yield

    ba = ba_ref[...]
    beta_all = jax.nn.sigmoid(ba)
    xs = ba + par_ref[1:2, :]
    softplus = jnp.maximum(xs, 0.0) + jnp.log(1.0 + jnp.exp(-jnp.abs(xs)))
    g_all = -jnp.exp(par_ref[0:1, :]) * softplus
    gc_of = [jnp.dot(ltri_ref[...], jnp.concatenate(_split3(g_all[p * PAIR:(p + 1) * PAIR, :]), axis=0),
                     preferred_element_type=F32) for p in range(ts // PAIR)]

    row = lax.broadcasted_iota(jnp.int32, (PAIR, PAIR), 0)
    col = lax.broadcasted_iota(jnp.int32, (PAIR, PAIR), 1)
    same = (row >> 6) == (col >> 6)
    lower = same & (row >= col)
    strict = same & (row > col)
    rcol = lax.broadcasted_iota(jnp.int32, (PAIR, 1), 0)

    def chunk_local(w):
        pairs = range(w * (DN_WAVE // PAIR), (w + 1) * (DN_WAVE // PAIR))
        units = [(p, h) for p in pairs for h in heads]
        rows_of = [slice(p * PAIR, (p + 1) * PAIR) for p, _ in units]
        gc_t = {p: gc_of[p].T for p in pairs}
        q = [act_ref[rs, h * DN_DK:(h + 1) * DN_DK] for rs, (_, h) in zip(rows_of, units)]
        k = [act_ref[rs, nqk + h * DN_DK:nqk + (h + 1) * DN_DK] for rs, (_, h) in zip(rows_of, units)]
        v = [act_ref[rs, 2 * nqk + h * DN_DV:2 * nqk + (h + 1) * DN_DV] for rs, (_, h) in zip(rows_of, units)]
        beta = [beta_all[rs, h:h + 1] for rs, (_, h) in zip(rows_of, units)]
        gcc = [gc_of[p][:, DN_HEADS + h:DN_HEADS + h + 1] for p, h in units]
        decay = [jnp.where(lower, jnp.exp(g - gc_t[p][DN_HEADS + h:DN_HEADS + h + 1, :]), 0.0)
                 for g, (p, h) in zip(gcc, units)]
        eg = [jnp.exp(g) for g in gcc]
        gend = [jnp.where(rcol < CHUNK, g[CHUNK - 1:CHUNK, :], g[PAIR - 1:PAIR, :]) for g in gcc]
        kb = [x.astype(BF16) for x in k]
        yield
        qkk = [lax.dot_general(jnp.concatenate([qq.astype(BF16), kk], axis=0), kk,
                               (((1,), (1,)), ((), ())), preferred_element_type=F32) for qq, kk in zip(q, kb)]
        yield
        ms = [jnp.where(strict, b * x[PAIR:] * d, 0.0) for b, x, d in zip(beta, qkk, decay)]
        tinv = yield from _tri_inverse(ms, row, col)
        sol = [jnp.dot(t.astype(BF16),
                       jnp.concatenate([(b * vv).astype(BF16), (b * e * kk).astype(BF16)], axis=1),
                       preferred_element_type=F32) for t, b, vv, e, kk in zip(tinv, beta, v, eg, k)]
        yield
        solb = [x.astype(BF16) for x in sol]
        qkb = [jnp.where(lower, x[:PAIR] * d, 0.0).astype(BF16) for x, d in zip(qkk, decay)]
        qwu = [jnp.dot(a, b, preferred_element_type=F32) for a, b in zip(qkb, solb)]
        yield
        kd = [kk * jnp.exp(ge - g) for kk, ge, g in zip(k, gend, gcc)]
        for c in range(PAIR // CHUNK):
            kdc = [jnp.where((row >> 6) == c, x, 0.0).astype(BF16) for x in kd]
            kwu = [lax.dot_general(a, b, (((0,), (0,)), ((), ())), preferred_element_type=F32)
                   for a, b in zip(kdc, solb)]
            yield
            for (p, h), x, g in zip(units, kwu, gcc):
                cc = p * (PAIR // CHUNK) + c
                bc_ref[h, cc] = x[:, :DN_DV]
                mc_ref[h, cc] = x[:, DN_DV:].astype(BF16)
                ge_ref[h, cc] = jnp.broadcast_to(jnp.exp(g[(c + 1) * CHUNK - 1:(c + 1) * CHUNK, :]), (8, LANES))
        for (p, h), rs, x, qq, e in zip(units, rows_of, qwu, q, eg):
            ol_ref[h, rs, :] = x[:, :DN_DV]
            qt_ref[h, rs, :] = (qq * e - x[:, DN_DV:]).astype(BF16)
        yield

    gain = gain_ref[...]
    st = [state_ref[h] for h in heads]

    def recurrence(w):
        for cc in range(w * (DN_WAVE // CHUNK), (w + 1) * (DN_WAVE // CHUNK)):
            cr = slice(cc * CHUNK, (cc + 1) * CHUNK)
            sb = [x.astype(BF16) for x in st]
            o = [ol_ref[h, cr, :] + jnp.dot(qt_ref[h, cr, :], sb[h], preferred_element_type=F32) for h in heads]
            new = [st[h] * ge_ref[h, cc, 0:1, :] + bc_ref[h, cc]
                   - jnp.dot(mc_ref[h, cc], sb[h], preferred_element_type=F32) for h in heads]
            for h in heads:
                st[h] = new[h]
            yield
            for h in heads:
                hs = slice(h * DN_DV, (h + 1) * DN_DV)
                on = o[h] * lax.rsqrt(jnp.mean(o[h] * o[h], axis=-1, keepdims=True) + EPS) * gain
                zz = z_ref[cr, hs].astype(F32)
                o_ref[cr, hs] = (on * _silu(zz)).astype(o_ref.dtype)
            yield

    _round_robin(activations(0))
    for w in range(nwave):
        others = []
        if w + 1 < nwave:
            others.append(activations(w + 1))
        if w > 0:
            others.append(recurrence(w - 1))
        _round_robin(chunk_local(w), *others)
    _round_robin(recurrence(nwave - 1))
    for h in heads:
        state_ref[h] = st[h]
    for s in range(nslab):
        buf_ref[s, 0:8, :] = buf_ref[s, ts:ts + 8, :]


def _deltanet(proj, ba, conv_w, par, gain, ltri3, batch, seq):
    nsb = seq // DN_TS
    nchunk = DN_TS // CHUNK
    return pl.pallas_call(
        _deltanet_kernel,
        grid=(batch, nsb),
        in_specs=[
            pl.BlockSpec((DN_TS, A_QKV), lambda b, s: (b * nsb + s, 0)),
            pl.BlockSpec((DN_TS, DN_WIDTH), lambda b, s: (b * nsb + s, A_QKV // DN_WIDTH)),
            pl.BlockSpec((DN_TS, LANES), lambda b, s: (b * nsb + s, 0)),
            pl.BlockSpec((DN_CONV, A_QKV), lambda b, s: (0, 0)),
            pl.BlockSpec((8, LANES), lambda b, s: (0, 0)),
            pl.BlockSpec((1, DN_DV), lambda b, s: (0, 0)),
            pl.BlockSpec((PAIR, 3 * PAIR), lambda b, s: (0, 0)),
        ],
        out_specs=pl.BlockSpec((DN_TS, DN_WIDTH), lambda b, s: (b * nsb + s, 0)),
        out_shape=jax.ShapeDtypeStruct((batch * seq, DN_WIDTH), BF16),
        scratch_shapes=[
            pltpu.VMEM((A_QKV // LANES, DN_TS + 8, LANES), F32),
            pltpu.VMEM((DN_TS, A_QKV), F32),
            pltpu.VMEM((DN_HEADS, DN_DK, DN_DV), F32),
            pltpu.VMEM((DN_HEADS, DN_TS, DN_DV), F32),
            pltpu.VMEM((DN_HEADS, DN_TS, DN_DK), BF16),
            pltpu.VMEM((DN_HEADS, nchunk, DN_DK, DN_DK), BF16),
            pltpu.VMEM((DN_HEADS, nchunk, DN_DK, DN_DV), F32),
            pltpu.VMEM((DN_HEADS, nchunk, 8, LANES), F32),
        ],
        compiler_params=pltpu.CompilerParams(
            dimension_semantics=("arbitrary", "arbitrary"), vmem_limit_bytes=VMEM_LIMIT),
        name="deltanet",
    )(proj, proj, ba, conv_w, par, gain, ltri3)


def _rms_halves(x, half_ones2, gain):
    hi, lo = _split2(x * x)
    ss = jnp.dot(jnp.concatenate([hi, lo], axis=1), half_ones2, preferred_element_type=F32)
    return x * lax.rsqrt(ss + SB_DH * EPS) * (gain * SB_DH ** 0.5)


def _sb_block(i, k_ref, cum_ref, o_ref, q2_ref, kn_ref, v2_ref, r_ref, acc_ref, flag_ref):
    npair = k_ref.shape[1] // LANES
    pairs = range(npair)

    def cols(p):
        return slice(p * LANES, (p + 1) * LANES)

    q2 = [q2_ref[p, i] for p in pairs]

    def scores(rows, mask):
        zs = [lax.dot_general(q2[p], kn_ref[rows, cols(p)], (((1,), (1,)), ((), ())),
                              preferred_element_type=F32) for p in pairs]
        lss = [jnp.minimum(z, 0.0) - jnp.log(1.0 + jnp.exp2(-jnp.abs(z))) * LOG2E for z in zs]
        lks = [ls - z for ls, z in zip(lss, zs)]
        if mask is not None:
            lks = [jnp.where(mask, lk, 0.0) for lk in lks]
        return lss, [jnp.concatenate(_split2(lk), axis=1) for lk in lks]

    def cumsums(lk2s):
        return [jnp.dot(lk2, cum_ref[...], preferred_element_type=F32) for lk2 in lk2s]

    def weights(lss, css, rs, mask):
        if rs is None:
            avs = [jnp.exp2(ls + cs[:, :SB_BLOCK]) for ls, cs in zip(lss, css)]
        else:
            avs = [jnp.exp2(ls + cs[:, :SB_BLOCK] + r) for ls, cs, r in zip(lss, css, rs)]
        if mask is not None:
            avs = [jnp.where(mask, a, 0.0) for a in avs]
        out = []
        for a in avs:
            ab = a.astype(BF16)
            out.append(jnp.concatenate([ab[:SB_BLOCK], ab[SB_BLOCK:]], axis=1))
        return out

    def tile_rows(j):
        return pl.ds(pl.multiple_of(j * SB_BLOCK, SB_BLOCK), SB_BLOCK)

    def sweep(tiles, masks, rs, accs):
        sc = [scores(tile_rows(j), m) for j, m in zip(tiles, masks)]
        css = [cumsums(lk2s) for _, lk2s in sc]
        for j, m, (lss, _), cs in zip(tiles, masks, sc, css):
            a2s = weights(lss, cs, rs, m)
            rs = [c[:, SB_BLOCK:] for c in cs] if rs is None else [r + c[:, SB_BLOCK:] for r, c in zip(rs, cs)]
            accs = [acc + jnp.dot(a2, v2_ref[p, j], preferred_element_type=F32)
                    for p, (acc, a2) in enumerate(zip(accs, a2s))]
        return rs, accs

    def load_state():
        return [r_ref[p] for p in pairs], [acc_ref[:, cols(p)] for p in pairs]

    def store_state(rs, accs):
        for p in pairs:
            r_ref[p] = rs[p]
            acc_ref[:, cols(p)] = accs[p]

    qrow = lax.broadcasted_iota(jnp.int32, (2 * SB_BLOCK, SB_BLOCK), 0) & (SB_BLOCK - 1)
    kcol = lax.broadcasted_iota(jnp.int32, (2 * SB_BLOCK, SB_BLOCK), 1)
    causal = kcol < qrow
    zero_accs = [jnp.zeros((SB_BLOCK, LANES), F32)] * npair

    def alive_flag(rs):
        return (jnp.max(functools.reduce(jnp.maximum, rs)) >= ZERO_WEIGHT_LOG2).astype(jnp.int32)

    def head_rows(x, lo, hi):
        return jnp.concatenate([x[lo:hi], x[SB_BLOCK + lo:SB_BLOCK + hi]], axis=0)

    @pl.when(i == 0)
    def _():
        rs, accs = sweep([i], [causal], None, zero_accs)
        store_state(rs, accs)
        flag_ref[0] = alive_flag(rs)
        flag_ref[1] = jnp.int32(0)

    @pl.when(i == 1)
    def _():
        rs, accs = sweep([i, i - 1], [causal, None], None, zero_accs)
        store_state(rs, accs)
        flag_ref[0] = alive_flag(rs)
        flag_ref[1] = jnp.int32(0)

    @pl.when(i >= 2)
    def _():
        j3 = i - 2
        sc = [scores(tile_rows(j), m) for j, m in ((i, causal), (i - 1, None))]
        q2n = [head_rows(q, 0, SB_NARROW) for q in q2]
        zs = [lax.dot_general(q2n[p], kn_ref[tile_rows(j3), cols(p)], (((1,), (1,)), ((), ())),
                              preferred_element_type=F32) for p in pairs]
        lss_n = [jnp.minimum(z, 0.0) - jnp.log(1.0 + jnp.exp2(-jnp.abs(z))) * LOG2E for z in zs]
        lk2_n = [jnp.concatenate(_split2(ls - z), axis=1) for ls, z in zip(lss_n, zs)]
        css = [cumsums(lk2s) for _, lk2s in sc]
        css_n = cumsums(lk2_n)
        rs, accs = None, zero_accs
        for j, m, (lss, _), cs in zip((i, i - 1), (causal, None), sc, css):
            a2s = weights(lss, cs, rs, m)
            rs = [c[:, SB_BLOCK:] for c in cs] if rs is None else [r + c[:, SB_BLOCK:] for r, c in zip(rs, cs)]
            accs = [acc + jnp.dot(a2, v2_ref[p, j], preferred_element_type=F32)
                    for p, (acc, a2) in enumerate(zip(accs, a2s))]
        rest = [head_rows(r, SB_NARROW, SB_BLOCK) for r in rs]
        rs_n = [head_rows(r, 0, SB_NARROW) for r in rs]
        a_n = [jnp.exp2(ls + cs[:, :SB_BLOCK] + r).astype(BF16) for ls, cs, r in zip(lss_n, css_n, rs_n)]
        top = [accs[p][0:SB_NARROW]
               + jnp.dot(jnp.concatenate([a_n[p][:SB_NARROW], a_n[p][SB_NARROW:]], axis=1), v2_ref[p, j3],
                         preferred_element_type=F32) for p in pairs]
        rs_n = [r + cs[:, SB_BLOCK:] for r, cs in zip(rs_n, css_n)]
        store_state(rs, accs)
        for p in pairs:
            r_ref[p, 0:SB_NARROW, :] = rs_n[p][:SB_NARROW]
            r_ref[p, SB_BLOCK:SB_BLOCK + SB_NARROW, :] = rs_n[p][SB_NARROW:]
            acc_ref[0:SB_NARROW, cols(p)] = top[p]
        flag_ref[1] = alive_flag(rest)
        flag_ref[0] = jnp.maximum(alive_flag(rest), alive_flag(rs_n))

    @pl.when(flag_ref[1] > 0)
    def _():
        rs, accs = sweep([i - 2], [qrow >= SB_NARROW], *load_state())
        store_state(rs, accs)
        flag_ref[0] = alive_flag(rs)

    def cond(carry):
        t, alive = carry
        return jnp.logical_and(t < i, alive > 0)

    def body(carry):
        t, _ = carry
        j = i - 1 - t
        rs, accs = sweep([j], [None], *load_state())
        store_state(rs, accs)
        return t + 1, alive_flag(rs)
    lax.while_loop(cond, body, (jnp.int32(SB_LEAD - 1), flag_ref[0]))

    o_ref[...] = acc_ref[...].astype(o_ref.dtype)


def _sb_kernel(q_ref, k_ref, v_ref, qg_ref, kg_ref, cum_ref, half_ref, o_ref,
               q2_ref, kn_ref, v2_ref, r_ref, acc_ref, flag_ref):
    step = pl.program_id(1)
    nblk = k_ref.shape[0] // SB_BLOCK
    pairs = range(k_ref.shape[1] // LANES)
    first = lax.broadcasted_iota(jnp.int32, (SB_BLOCK, LANES), 1) < SB_DH

    def cols(p):
        return slice(p * LANES, (p + 1) * LANES)

    @pl.when(step == 0)
    def _():
        qscale = qg_ref[...] * (SB_DH ** -0.5 * LOG2E)

        def prep(j, carry):
            rows = pl.ds(pl.multiple_of(j * SB_BLOCK, SB_BLOCK), SB_BLOCK)
            for p in pairs:
                qn = _rms_halves(q_ref[rows, cols(p)].astype(F32), half_ref[...], qscale).astype(BF16)
                zero = jnp.zeros_like(qn)
                q2_ref[p, j, 0:SB_BLOCK, :] = jnp.where(first, qn, zero)
                q2_ref[p, j, SB_BLOCK:2 * SB_BLOCK, :] = jnp.where(first, zero, qn)
                kn = _rms_halves(k_ref[rows, cols(p)].astype(F32), half_ref[...], kg_ref[...])
                kn_ref[rows, cols(p)] = kn.astype(BF16)
                vb = v_ref[rows, cols(p)]
                v2_ref[p, j, 0:SB_BLOCK, :] = jnp.where(first, vb, zero)
                v2_ref[p, j, SB_BLOCK:2 * SB_BLOCK, :] = jnp.where(first, zero, vb)
            return carry
        lax.fori_loop(0, nblk, prep, 0)

    def block(sub, carry):
        out_rows = pl.ds(pl.multiple_of(sub * SB_BLOCK, SB_BLOCK), SB_BLOCK)
        _sb_block(step * SB_QB + sub, k_ref, cum_ref, o_ref.at[out_rows, :],
                  q2_ref, kn_ref, v2_ref, r_ref, acc_ref, flag_ref)
        return carry
    lax.fori_loop(0, SB_QB, block, 0)


def _stick_breaking(proj, qg2, kg2, cum_mat, half_ones2, batch, seq):
    nq = seq // SB_BLOCK
    npair = SB_HEADS // 2
    qcol = (A_QKV + DN_WIDTH) // SB_WIDTH
    return pl.pallas_call(
        _sb_kernel,
        grid=(batch, nq // SB_QB),
        in_specs=[
            pl.BlockSpec((seq, SB_WIDTH), lambda b, i: (b, qcol)),
            pl.BlockSpec((seq, SB_WIDTH), lambda b, i: (b, qcol + 1)),
            pl.BlockSpec((seq, SB_WIDTH), lambda b, i: (b, qcol + 2)),
            pl.BlockSpec((1, LANES), lambda b, i: (0, 0)),
            pl.BlockSpec((1, LANES), lambda b, i: (0, 0)),
            pl.BlockSpec((2 * SB_BLOCK, 2 * SB_BLOCK), lambda b, i: (0, 0)),
            pl.BlockSpec((2 * LANES, LANES), lambda b, i: (0, 0)),
        ],
        out_specs=pl.BlockSpec((SB_QB * SB_BLOCK, SB_WIDTH), lambda b, i: (b * (nq // SB_QB) + i, 0)),
        out_shape=jax.ShapeDtypeStruct((batch * seq, SB_WIDTH), BF16),
        scratch_shapes=[
            pltpu.VMEM((npair, nq, 2 * SB_BLOCK, LANES), BF16),
            pltpu.VMEM((seq, SB_WIDTH), BF16),
            pltpu.VMEM((npair, nq, 2 * SB_BLOCK, LANES), BF16),
            pltpu.VMEM((npair, 2 * SB_BLOCK, SB_BLOCK), F32),
            pltpu.VMEM((SB_BLOCK, SB_WIDTH), F32),
            pltpu.SMEM((2,), jnp.int32),
        ],
        compiler_params=pltpu.CompilerParams(
            dimension_semantics=("arbitrary", "arbitrary"), vmem_limit_bytes=VMEM_LIMIT),
        name="stickbreak",
    )(proj, proj, proj, qg2, kg2, cum_mat, half_ones2)


def _merge_kernel(x_ref, gain_ref, oa_ref, ob_ref, zb_ref, wg_ref, bg_ref, wa_ref, wb_ref, wo_ref, out_ref):
    x = x_ref[...]
    xn = (x * lax.rsqrt(jnp.mean(x * x, axis=-1, keepdims=True) + EPS) * gain_ref[...]).astype(BF16)
    gates = jax.nn.sigmoid(jnp.dot(xn, wg_ref[...], preferred_element_type=F32) + bg_ref[...])
    zb = zb_ref[...].astype(F32)
    ob = (ob_ref[...].astype(F32) * _silu(zb)).astype(BF16)
    ya = jnp.dot(oa_ref[...], wa_ref[...], preferred_element_type=F32)
    yb = jnp.dot(ob, wb_ref[...], preferred_element_type=F32)
    merged = gates[:, :D_MODEL] * ya + gates[:, D_MODEL:] * yb
    out_ref[...] = x + jnp.dot(merged.astype(BF16), wo_ref[...], preferred_element_type=F32)


def _merge(x2, gain, o_a, o_b, proj, w_gate, b_gate, w_up_a, w_up_b, w_out):
    t = x2.shape[0]
    zb_col = (A_QKV + DN_WIDTH + 3 * SB_WIDTH) // SB_WIDTH
    const = lambda i: (0, 0)
    return pl.pallas_call(
        _merge_kernel,
        grid=(t // OUT_TM,),
        in_specs=[
            pl.BlockSpec((OUT_TM, D_MODEL), lambda i: (i, 0)),
            pl.BlockSpec((1, D_MODEL), const),
            pl.BlockSpec((OUT_TM, DN_WIDTH), lambda i: (i, 0)),
            pl.BlockSpec((OUT_TM, SB_WIDTH), lambda i: (i, 0)),
            pl.BlockSpec((OUT_TM, SB_WIDTH), lambda i: (i, zb_col)),
            pl.BlockSpec((D_MODEL, 2 * D_MODEL), const, pipeline_mode=pl.Buffered(1)),
            pl.BlockSpec((1, 2 * D_MODEL), const),
            pl.BlockSpec((DN_WIDTH, D_MODEL), const, pipeline_mode=pl.Buffered(1)),
            pl.BlockSpec((SB_WIDTH, D_MODEL), const, pipeline_mode=pl.Buffered(1)),
            pl.BlockSpec((D_MODEL, D_MODEL), const, pipeline_mode=pl.Buffered(1)),
        ],
        out_specs=pl.BlockSpec((OUT_TM, D_MODEL), lambda i: (i, 0)),
        out_shape=jax.ShapeDtypeStruct((t, D_MODEL), F32),
        compiler_params=pltpu.CompilerParams(
            dimension_semantics=("arbitrary",), vmem_limit_bytes=VMEM_LIMIT),
        name="merge",
    )(x2, gain, o_a, o_b, proj, w_gate, b_gate, w_up_a, w_up_b, w_out)


def _layer(x, norm_gain, w_in, b_gate, conv_w, a_log, dt_bias, dn_out_gain,
           sb_q_gain, sb_k_gain, w_up_a, w_up_b, w_out):
    batch, seq, _ = x.shape
    x2 = x.reshape(batch * seq, D_MODEL)

    w_main = jnp.concatenate([w_in[:, :BA_COL0], w_in[:, BA_COL0 + BA_COLS:GATE_COL0]], axis=1).astype(BF16)
    w_gate = w_in[:, GATE_COL0:].astype(BF16)
    w_ba = jnp.pad(w_in[:, BA_COL0:BA_COL0 + BA_COLS], ((0, 0), (0, LANES - BA_COLS)))
    wba_hi = w_ba.astype(BF16)
    wba_lo = (w_ba - wba_hi.astype(F32)).astype(BF16)
    par = jnp.zeros((8, LANES), F32)
    par = par.at[0, DN_HEADS:2 * DN_HEADS].set(a_log).at[1, DN_HEADS:2 * DN_HEADS].set(dt_bias)
    qg2 = jnp.concatenate([sb_q_gain, sb_q_gain])[None, :]
    kg2 = jnp.concatenate([sb_k_gain, sb_k_gain])[None, :]
    kk = jnp.arange(SB_BLOCK)
    cum1 = jnp.concatenate([(kk[:, None] > kk[None, :]).astype(BF16), jnp.ones((SB_BLOCK, SB_BLOCK), BF16)], axis=1)
    cum_mat = jnp.concatenate([cum1, cum1], axis=0)
    ln = jnp.arange(LANES)
    half1 = ((ln[:, None] // SB_DH) == (ln[None, :] // SB_DH)).astype(BF16)
    half_ones2 = jnp.concatenate([half1, half1], axis=0)
    tt = jnp.arange(PAIR)
    ltri1 = ((tt[:, None] // CHUNK == tt[None, :] // CHUNK) & (tt[:, None] >= tt[None, :])).astype(BF16)
    ltri3 = jnp.concatenate([ltri1, ltri1, ltri1], axis=1)

    proj, ba = _inproj(x2, norm_gain[None, :], w_main, wba_hi, wba_lo)
    o_a = _deltanet(proj, ba, conv_w, par, dn_out_gain[None, :], ltri3, batch, seq)
    o_b = _stick_breaking(proj, qg2, kg2, cum_mat, half_ones2, batch, seq)
    out = _merge(x2, norm_gain[None, :], o_a, o_b, proj, w_gate, b_gate[None, :], w_up_a.astype(BF16),
                 w_up_b.astype(BF16), w_out.astype(BF16))
    return out.reshape(batch, seq, D_MODEL)


def kernel(x, norm_gain, w_in, b_gate, conv_w, a_log, dt_bias, dn_out_gain, sb_q_gain, sb_k_gain,
           w_up_a, w_up_b, w_out):
    h = x
    for layer in range(norm_gain.shape[0]):
        h = _layer(h, norm_gain[layer], w_in[layer], b_gate[layer], conv_w[layer], a_log[layer],
                   dt_bias[layer], dn_out_gain[layer], sb_q_gain[layer], sb_k_gain[layer],
                   w_up_a[layer], w_up_b[layer], w_out[layer])
    return h
```

```python
import functools

import jax
import jax.numpy as jnp
from jax import lax
from jax.experimental import pallas as pl
from jax.experimental.pallas import tpu as pltpu

F32 = jnp.float32
BF16 = jnp.bfloat16

D_MODEL = 1024
CHUNK = 64
SB_BLOCK = 128
EPS = 1e-6
LOG2E = 1.4426950408889634
ZERO_WEIGHT_LOG2 = -151.0
SB_LEAD = 3
SB_QB = 16
SB_NARROW = 32
DN_HEADS = 4
DN_DK = 128
DN_DV = 128
DN_CONV = 4
SB_HEADS = 8
SB_DH = 64
DN_WIDTH = DN_HEADS * DN_DV
SB_WIDTH = SB_HEADS * SB_DH
A_QKV = 2 * DN_HEADS * DN_DK + DN_WIDTH
BA_COL0 = A_QKV + DN_WIDTH
BA_COLS = 2 * DN_HEADS
MAIN_WIDTH = A_QKV + DN_WIDTH + 3 * SB_WIDTH + SB_WIDTH
GATE_COL0 = BA_COL0 + BA_COLS + 4 * SB_WIDTH

LANES = 128
PAIR = 2 * CHUNK

IN_TM, IN_TN = 1024, 4096
DN_TS = 1024
DN_WAVE = 256
OUT_TM = 1024

VMEM_LIMIT = 48 * 1024 * 1024


def _split3(x):
    hi = x.astype(BF16)
    r1 = x - hi.astype(F32)
    mid = r1.astype(BF16)
    lo = (r1 - mid.astype(F32)).astype(BF16)
    return hi, mid, lo


def _split2(x):
    hi = x.astype(BF16)
    lo = (x - hi.astype(F32)).astype(BF16)
    return hi, lo


def _silu(x):
    h = 0.5 * x
    return h + h * jnp.tanh(h)


def _inproj_kernel(x_ref, gain_ref, w_ref, wba_hi_ref, wba_lo_ref, proj_ref, ba_ref, xn_ref):
    j = pl.program_id(1)

    @pl.when(j == 0)
    def _():
        half = x_ref.shape[0] // 2
        for r in range(2):
            rows = slice(r * half, (r + 1) * half)
            x = x_ref[rows, :]
            xn = x * lax.rsqrt(jnp.mean(x * x, axis=-1, keepdims=True) + EPS) * gain_ref[...]
            hi, lo = _split2(xn)
            xn_ref[rows, :] = hi
            ba_ref[rows, :] = (jnp.dot(hi, wba_hi_ref[...], preferred_element_type=F32)
                               + jnp.dot(lo, wba_hi_ref[...], preferred_element_type=F32)
                               + jnp.dot(hi, wba_lo_ref[...], preferred_element_type=F32))
            proj_ref[rows, :] = jnp.dot(hi, w_ref[...], preferred_element_type=F32).astype(proj_ref.dtype)

    @pl.when(j != 0)
    def _():
        proj_ref[...] = jnp.dot(xn_ref[...], w_ref[...], preferred_element_type=F32).astype(proj_ref.dtype)


def _inproj(x2, gain, w_main, wba_hi, wba_lo):
    t = x2.shape[0]
    return pl.pallas_call(
        _inproj_kernel,
        grid=(t // IN_TM, MAIN_WIDTH // IN_TN),
        in_specs=[
            pl.BlockSpec((IN_TM, D_MODEL), lambda i, j: (i, 0)),
            pl.BlockSpec((1, D_MODEL), lambda i, j: (0, 0)),
            pl.BlockSpec((D_MODEL, IN_TN), lambda i, j: (0, j), pipeline_mode=pl.Buffered(1)),
            pl.BlockSpec((D_MODEL, LANES), lambda i, j: (0, 0)),
            pl.BlockSpec((D_MODEL, LANES), lambda i, j: (0, 0)),
        ],
        out_specs=[
            pl.BlockSpec((IN_TM, IN_TN), lambda i, j: (i, j)),
            pl.BlockSpec((IN_TM, LANES), lambda i, j: (i, 0)),
        ],
        out_shape=[
            jax.ShapeDtypeStruct((t, MAIN_WIDTH), BF16),
            jax.ShapeDtypeStruct((t, LANES), F32),
        ],
        scratch_shapes=[pltpu.VMEM((IN_TM, D_MODEL), BF16)],
        compiler_params=pltpu.CompilerParams(
            dimension_semantics=("arbitrary", "arbitrary"), vmem_limit_bytes=VMEM_LIMIT),
        name="inproj",
    )(x2, gain, w_main, wba_hi, wba_lo)


def _tri_inverse(ms, row, col):
    blk8 = (row >> 3) == (col >> 3)
    eye = jnp.where(row == col, 1.0, 0.0)
    pds = [jnp.where(blk8, -m, 0.0) for m in ms]
    pdb = [pd.astype(BF16) for pd in pds]
    xs = [eye + pd for pd in pds]
    p2b = [jnp.dot(b, b, preferred_element_type=F32).astype(BF16) for b in pdb]
    yield
    xs = [x + jnp.dot(x.astype(BF16), p2, preferred_element_type=F32) for x, p2 in zip(xs, p2b)]
    yield
    p4b = [jnp.dot(b, b, preferred_element_type=F32).astype(BF16) for b in p2b]
    yield
    xs = [x + jnp.dot(x.astype(BF16), p4, preferred_element_type=F32) for x, p4 in zip(xs, p4b)]
    yield
    for s in (8, 16, 32):
        sh = s.bit_length()
        off = ((row >> sh) == (col >> sh)) & ((row & (2 * s - 1)) >= s) & ((col & (2 * s - 1)) < s)
        xb = [x.astype(BF16) for x in xs]
        ts_ = [jnp.dot(b, jnp.where(off, m, 0.0).astype(BF16), preferred_element_type=F32).astype(BF16)
               for b, m in zip(xb, ms)]
        yield
        xs = [x - jnp.dot(t, b, preferred_element_type=F32) for x, t, b in zip(xs, ts_, xb)]
        yield
    return xs


def _round_robin(*gens):
    live = list(gens)
    while live:
        for g in list(live):
            try:
                next(g)
            except StopIteration:
                live.remove(g)


def _deltanet_kernel(qkv_ref, z_ref, ba_ref, convw_ref, par_ref, gain_ref, ltri_ref, o_ref,
                     buf_ref, act_ref, state_ref, ol_ref, qt_ref, mc_ref, bc_ref, ge_ref):
    ts = qkv_ref.shape[0]
    nqk = DN_HEADS * DN_DK
    heads = range(DN_HEADS)
    nslab = A_QKV // LANES
    nwave = ts // DN_WAVE

    @pl.when(pl.program_id(1) == 0)
    def _():
        buf_ref[:, 0:8, :] = jnp.zeros((nslab, 8, LANES), F32)
        state_ref[...] = jnp.zeros_like(state_ref)

    for s in range(nslab):
        buf_ref[s, 8:8 + ts, :] = qkv_ref[:, s * LANES:(s + 1) * LANES].astype(F32)

    def activations(w):
        for r0 in range(w * DN_WAVE, (w + 1) * DN_WAVE, PAIR):
            for s in range(nslab):
                cs = slice(s * LANES, (s + 1) * LANES)
                acc = None
                for i in range(DN_CONV):
                    lo = 8 - (DN_CONV - 1) + i + r0
                    term = buf_ref[s, lo:lo + PAIR, :] * convw_ref[i:i + 1, cs]
                    acc = term if acc is None else acc + term
                y = _silu(acc)
                if s < 2 * DN_HEADS:
                    y = y * lax.rsqrt(jnp.sum(y * y, axis=-1, keepdims=True) + EPS)
                if s < DN_HEADS:
                    y = y * (DN_DK ** -0.5)
                act_ref[r0:r0 + PAIR, cs] = y
                yield

    ba = ba_ref[...]
    beta_all = jax.nn.sigmoid(ba)
    xs = ba + par_ref[1:2, :]
    softplus = jnp.maximum(xs, 0.0) + jnp.log(1.0 + jnp.exp(-jnp.abs(xs)))
    g_all = -jnp.exp(par_ref[0:1, :]) * softplus
    gc_of = [jnp.dot(ltri_ref[...], jnp.concatenate(_split3(g_all[p * PAIR:(p + 1) * PAIR, :]), axis=0),
                     preferred_element_type=F32) for p in range(ts // PAIR)]

    row = lax.broadcasted_iota(jnp.int32, (PAIR, PAIR), 0)
    col = lax.broadcasted_iota(jnp.int32, (PAIR, PAIR), 1)
    same = (row >> 6) == (col >> 6)
    lower = same & (row >= col)
    strict = same & (row > col)
    rcol = lax.broadcasted_iota(jnp.int32, (PAIR, 1), 0)

    def chunk_local(w):
        pairs = range(w * (DN_WAVE // PAIR), (w + 1) * (DN_WAVE // PAIR))
        units = [(p, h) for p in pairs for h in heads]
        rows_of = [slice(p * PAIR, (p + 1) * PAIR) for p, _ in units]
        gc_t = {p: gc_of[p].T for p in pairs}
        q = [act_ref[rs, h * DN_DK:(h + 1) * DN_DK] for rs, (_, h) in zip(rows_of, units)]
        k = [act_ref[rs, nqk + h * DN_DK:nqk + (h + 1) * DN_DK] for rs, (_, h) in zip(rows_of, units)]
        v = [act_ref[rs, 2 * nqk + h * DN_DV:2 * nqk + (h + 1) * DN_DV] for rs, (_, h) in zip(rows_of, units)]
        beta = [beta_all[rs, h:h + 1] for rs, (_, h) in zip(rows_of, units)]
        gcc = [gc_of[p][:, DN_HEADS + h:DN_HEADS + h + 1] for p, h in units]
        decay = [jnp.where(lower, jnp.exp(g - gc_t[p][DN_HEADS + h:DN_HEADS + h + 1, :]), 0.0)
                 for g, (p, h) in zip(gcc, units)]
        eg = [jnp.exp(g) for g in gcc]
        gend = [jnp.where(rcol < CHUNK, g[CHUNK - 1:CHUNK, :], g[PAIR - 1:PAIR, :]) for g in gcc]
        kb = [x.astype(BF16) for x in k]
        yield
        qkk = [lax.dot_general(jnp.concatenate([qq.astype(BF16), kk], axis=0), kk,
                               (((1,), (1,)), ((), ())), preferred_element_type=F32) for qq, kk in zip(q, kb)]
        yield
        ms = [jnp.where(strict, b * x[PAIR:] * d, 0.0) for b, x, d in zip(beta, qkk, decay)]
        tinv = yield from _tri_inverse(ms, row, col)
        sol = [jnp.dot(t.astype(BF16),
                       jnp.concatenate([(b * vv).astype(BF16), (b * e * kk).astype(BF16)], axis=1),
                       preferred_element_type=F32) for t, b, vv, e, kk in zip(tinv, beta, v, eg, k)]
        yield
        solb = [x.astype(BF16) for x in sol]
        qkb = [jnp.where(lower, x[:PAIR] * d, 0.0).astype(BF16) for x, d in zip(qkk, decay)]
        qwu = [jnp.dot(a, b, preferred_element_type=F32) for a, b in zip(qkb, solb)]
        yield
        kd = [kk * jnp.exp(ge - g) for kk, ge, g in zip(k, gend, gcc)]
        for c in range(PAIR // CHUNK):
            kdc = [jnp.where((row >> 6) == c, x, 0.0).astype(BF16) for x in kd]
            kwu = [lax.dot_general(a, b, (((0,), (0,)), ((), ())), preferred_element_type=F32)
                   for a, b in zip(kdc, solb)]
            yield
            for (p, h), x, g in zip(units, kwu, gcc):
                cc = p * (PAIR // CHUNK) + c
                bc_ref[h, cc] = x[:, :DN_DV]
                mc_ref[h, cc] = x[:, DN_DV:].astype(BF16)
                ge_ref[h, cc] = jnp.broadcast_to(jnp.exp(g[(c + 1) * CHUNK - 1:(c + 1) * CHUNK, :]), (8, LANES))
        for (p, h), rs, x, qq, e in zip(units, rows_of, qwu, q, eg):
            ol_ref[h, rs, :] = x[:, :DN_DV]
            qt_ref[h, rs, :] = (qq * e - x[:, DN_DV:]).astype(BF16)
        yield

    gain = gain_ref[...]
    st = [state_ref[h] for h in heads]

    def recurrence(w):
        for cc in range(w * (DN_WAVE // CHUNK), (w + 1) * (DN_WAVE // CHUNK)):
            cr = slice(cc * CHUNK, (cc + 1) * CHUNK)
            sb = [x.astype(BF16) for x in st]
            o = [ol_ref[h, cr, :] + jnp.dot(qt_ref[h, cr, :], sb[h], preferred_element_type=F32) for h in heads]
            new = [st[h] * ge_ref[h, cc, 0:1, :] + bc_ref[h, cc]
                   - jnp.dot(mc_ref[h, cc], sb[h], preferred_element_type=F32) for h in heads]
            for h in heads:
                st[h] = new[h]
            yield
            for h in heads:
                hs = slice(h * DN_DV, (h + 1) * DN_DV)
                on = o[h] * lax.rsqrt(jnp.mean(o[h] * o[h], axis=-1, keepdims=True) + EPS) * gain
                zz = z_ref[cr, hs].astype(F32)
                o_ref[cr, hs] = (on * _silu(zz)).astype(o_ref.dtype)
            yield

    _round_robin(activations(0))
    for w in range(nwave):
        others = []
        if w + 1 < nwave:
            others.append(activations(w + 1))
        if w > 0:
            others.append(recurrence(w - 1))
        _round_robin(chunk_local(w), *others)
    _round_robin(recurrence(nwave - 1))
    for h in heads:
        state_ref[h] = st[h]
    for s in range(nslab):
        buf_ref[s, 0:8, :] = buf_ref[s, ts:ts + 8, :]


def _deltanet(proj, ba, conv_w, par, gain, ltri3, batch, seq):
    nsb = seq // DN_TS
    nchunk = DN_TS // CHUNK
    return pl.pallas_call(
        _deltanet_kernel,
        grid=(batch, nsb),
        in_specs=[
            pl.BlockSpec((DN_TS, A_QKV), lambda b, s: (b * nsb + s, 0)),
            pl.BlockSpec((DN_TS, DN_WIDTH), lambda b, s: (b * nsb + s, A_QKV // DN_WIDTH)),
            pl.BlockSpec((DN_TS, LANES), lambda b, s: (b * nsb + s, 0)),
            pl.BlockSpec((DN_CONV, A_QKV), lambda b, s: (0, 0)),
            pl.BlockSpec((8, LANES), lambda b, s: (0, 0)),
            pl.BlockSpec((1, DN_DV), lambda b, s: (0, 0)),
            pl.BlockSpec((PAIR, 3 * PAIR), lambda b, s: (0, 0)),
        ],
        out_specs=pl.BlockSpec((DN_TS, DN_WIDTH), lambda b, s: (b * nsb + s, 0)),
        out_shape=jax.ShapeDtypeStruct((batch * seq, DN_WIDTH), BF16),
        scratch_shapes=[
            pltpu.VMEM((A_QKV // LANES, DN_TS + 8, LANES), F32),
            pltpu.VMEM((DN_TS, A_QKV), F32),
            pltpu.VMEM((DN_HEADS, DN_DK, DN_DV), F32),
            pltpu.VMEM((DN_HEADS, DN_TS, DN_DV), F32),
            pltpu.VMEM((DN_HEADS, DN_TS, DN_DK), BF16),
            pltpu.VMEM((DN_HEADS, nchunk, DN_DK, DN_DK), BF16),
            pltpu.VMEM((DN_HEADS, nchunk, DN_DK, DN_DV), F32),
            pltpu.VMEM((DN_HEADS, nchunk, 8, LANES), F32),
        ],
        compiler_params=pltpu.CompilerParams(
            dimension_semantics=("arbitrary", "arbitrary"), vmem_limit_bytes=VMEM_LIMIT),
        name="deltanet",
    )(proj, proj, ba, conv_w, par, gain, ltri3)


def _rms_halves(x, half_ones2, gain):
    hi, lo = _split2(x * x)
    ss = jnp.dot(jnp.concatenate([hi, lo], axis=1), half_ones2, preferred_element_type=F32)
    return x * lax.rsqrt(ss + SB_DH * EPS) * (gain * SB_DH ** 0.5)


def _sb_block(i, k_ref, cum_ref, o_ref, q2_ref, kn_ref, v2_ref, r_ref, acc_ref, flag_ref):
    npair = k_ref.shape[1] // LANES
    pairs = range(npair)

    def cols(p):
        return slice(p * LANES, (p + 1) * LANES)

    q2 = [q2_ref[p, i] for p in pairs]

    def scores(rows, mask):
        zs = [lax.dot_general(q2[p], kn_ref[rows, cols(p)], (((1,), (1,)), ((), ())),
                              preferred_element_type=F32) for p in pairs]
        lss = [jnp.minimum(z, 0.0) - jnp.log(1.0 + jnp.exp2(-jnp.abs(z))) * LOG2E for z in zs]
        lks = [ls - z for ls, z in zip(lss, zs)]
        if mask is not None:
            lks = [jnp.where(mask, lk, 0.0) for lk in lks]
        return lss, [jnp.concatenate(_split2(lk), axis=1) for lk in lks]

    def cumsums(lk2s):
        return [jnp.dot(lk2, cum_ref[...], preferred_element_type=F32) for lk2 in lk2s]

    def weights(lss, css, rs, mask):
        if rs is None:
            avs = [jnp.exp2(ls + cs[:, :SB_BLOCK]) for ls, cs in zip(lss, css)]
        else:
            avs = [jnp.exp2(ls + cs[:, :SB_BLOCK] + r) for ls, cs, r in zip(lss, css, rs)]
        if mask is not None:
            avs = [jnp.where(mask, a, 0.0) for a in avs]
        out = []
        for a in avs:
            ab = a.astype(BF16)
            out.append(jnp.concatenate([ab[:SB_BLOCK], ab[SB_BLOCK:]], axis=1))
        return out

    def tile_rows(j):
        return pl.ds(pl.multiple_of(j * SB_BLOCK, SB_BLOCK), SB_BLOCK)

    def sweep(tiles, masks, rs, accs):
        sc = [scores(tile_rows(j), m) for j, m in zip(tiles, masks)]
        css = [cumsums(lk2s) for _, lk2s in sc]
        for j, m, (lss, _), cs in zip(tiles, masks, sc, css):
            a2s = weights(lss, cs, rs, m)
            rs = [c[:, SB_BLOCK:] for c in cs] if rs is None else [r + c[:, SB_BLOCK:] for r, c in zip(rs, cs)]
            accs = [acc + jnp.dot(a2, v2_ref[p, j], preferred_element_type=F32)
                    for p, (acc, a2) in enumerate(zip(accs, a2s))]
        return rs, accs

    def load_state():
        return [r_ref[p] for p in pairs], [acc_ref[:, cols(p)] for p in pairs]

    def store_state(rs, accs):
        for p in pairs:
            r_ref[p] = rs[p]
            acc_ref[:, cols(p)] = accs[p]

    qrow = lax.broadcasted_iota(jnp.int32, (2 * SB_BLOCK, SB_BLOCK), 0) & (SB_BLOCK - 1)
    kcol = lax.broadcasted_iota(jnp.int32, (2 * SB_BLOCK, SB_BLOCK), 1)
    causal = kcol < qrow
    zero_accs = [jnp.zeros((SB_BLOCK, LANES), F32)] * npair

    def alive_flag(rs):
        return (jnp.max(functools.reduce(jnp.maximum, rs)) >= ZERO_WEIGHT_LOG2).astype(jnp.int32)

    def head_rows(x, lo, hi):
        return jnp.concatenate([x[lo:hi], x[SB_BLOCK + lo:SB_BLOCK + hi]], axis=0)

    @pl.when(i == 0)
    def _():
        rs, accs = sweep([i], [causal], None, zero_accs)
        store_state(rs, accs)
        flag_ref[0] = alive_flag(rs)
        flag_ref[1] = jnp.int32(0)

    @pl.when(i == 1)
    def _():
        rs, accs = sweep([i, i - 1], [causal, None], None, zero_accs)
        store_state(rs, accs)
        flag_ref[0] = alive_flag(rs)
        flag_ref[1] = jnp.int32(0)

    @pl.when(i >= 2)
    def _():
        j3 = i - 2
        sc = [scores(tile_rows(j), m) for j, m in ((i, causal), (i - 1, None))]
        q2n = [head_rows(q, 0, SB_NARROW) for q in q2]
        zs = [lax.dot_general(q2n[p], kn_ref[tile_rows(j3), cols(p)], (((1,), (1,)), ((), ())),
                              preferred_element_type=F32) for p in pairs]
        lss_n = [jnp.minimum(z, 0.0) - jnp.log(1.0 + jnp.exp2(-jnp.abs(z))) * LOG2E for z in zs]
        lk2_n = [jnp.concatenate(_split2(ls - z), axis=1) for ls, z in zip(lss_n, zs)]
        css = [cumsums(lk2s) for _, lk2s in sc]
        css_n = cumsums(lk2_n)
        rs, accs = None, zero_accs
        for j, m, (lss, _), cs in zip((i, i - 1), (causal, None), sc, css):
            a2s = weights(lss, cs, rs, m)
            rs = [c[:, SB_BLOCK:] for c in cs] if rs is None else [r + c[:, SB_BLOCK:] for r, c in zip(rs, cs)]
            accs = [acc + jnp.dot(a2, v2_ref[p, j], preferred_element_type=F32)
                    for p, (acc, a2) in enumerate(zip(accs, a2s))]
        rest = [head_rows(r, SB_NARROW, SB_BLOCK) for r in rs]
        rs_n = [head_rows(r, 0, SB_NARROW) for r in rs]
        a_n = [jnp.exp2(ls + cs[:, :SB_BLOCK] + r).astype(BF16) for ls, cs, r in zip(lss_n, css_n, rs_n)]
        top = [accs[p][0:SB_NARROW]
               + jnp.dot(jnp.concatenate([a_n[p][:SB_NARROW], a_n[p][SB_NARROW:]], axis=1), v2_ref[p, j3],
                         preferred_element_type=F32) for p in pairs]
        rs_n = [r + cs[:, SB_BLOCK:] for r, cs in zip(rs_n, css_n)]
        store_state(rs, accs)
        for p in pairs:
            r_ref[p, 0:SB_NARROW, :] = rs_n[p][:SB_NARROW]
            r_ref[p, SB_BLOCK:SB_BLOCK + SB_NARROW, :] = rs_n[p][SB_NARROW:]
            acc_ref[0:SB_NARROW, cols(p)] = top[p]
        flag_ref[1] = alive_flag(rest)
        flag_ref[0] = jnp.maximum(alive_flag(rest), alive_flag(rs_n))

    @pl.when(flag_ref[1] > 0)
    def _():
        rs, accs = sweep([i - 2], [qrow >= SB_NARROW], *load_state())
        store_state(rs, accs)
        flag_ref[0] = alive_flag(rs)

    def cond(carry):
        t, alive = carry
        return jnp.logical_and(t < i, alive > 0)

    def body(carry):
        t, _ = carry
        j = i - 1 - t
        rs, accs = sweep([j], [None], *load_state())
        store_state(rs, accs)
        return t + 1, alive_flag(rs)
    lax.while_loop(cond, body, (jnp.int32(SB_LEAD - 1), flag_ref[0]))

    o_ref[...] = acc_ref[...].astype(o_ref.dtype)


def _sb_kernel(q_ref, k_ref, v_ref, qg_ref, kg_ref, cum_ref, half_ref, o_ref,
               q2_ref, kn_ref, v2_ref, r_ref, acc_ref, flag_ref):
    step = pl.program_id(1)
    nblk = k_ref.shape[0] // SB_BLOCK
    pairs = range(k_ref.shape[1] // LANES)
    first = lax.broadcasted_iota(jnp.int32, (SB_BLOCK, LANES), 1) < SB_DH

    def cols(p):
        return slice(p * LANES, (p + 1) * LANES)

    @pl.when(step == 0)
    def _():
        qscale = qg_ref[...] * (SB_DH ** -0.5 * LOG2E)

        def prep(j, carry):
            rows = pl.ds(pl.multiple_of(j * SB_BLOCK, SB_BLOCK), SB_BLOCK)
            for p in pairs:
                qn = _rms_halves(q_ref[rows, cols(p)].astype(F32), half_ref[...], qscale).astype(BF16)
                zero = jnp.zeros_like(qn)
                q2_ref[p, j, 0:SB_BLOCK, :] = jnp.where(first, qn, zero)
                q2_ref[p, j, SB_BLOCK:2 * SB_BLOCK, :] = jnp.where(first, zero, qn)
                kn = _rms_halves(k_ref[rows, cols(p)].astype(F32), half_ref[...], kg_ref[...])
                kn_ref[rows, cols(p)] = kn.astype(BF16)
                vb = v_ref[rows, cols(p)]
                v2_ref[p, j, 0:SB_BLOCK, :] = jnp.where(first, vb, zero)
                v2_ref[p, j, SB_BLOCK:2 * SB_BLOCK, :] = jnp.where(first, zero, vb)
            return carry
        lax.fori_loop(0, nblk, prep, 0)

    def block(sub, carry):
        out_rows = pl.ds(pl.multiple_of(sub * SB_BLOCK, SB_BLOCK), SB_BLOCK)
        _sb_block(step * SB_QB + sub, k_ref, cum_ref, o_ref.at[out_rows, :],
                  q2_ref, kn_ref, v2_ref, r_ref, acc_ref, flag_ref)
        return carry
    lax.fori_loop(0, SB_QB, block, 0)


def _stick_breaking(proj, qg2, kg2, cum_mat, half_ones2, batch, seq):
    nq = seq // SB_BLOCK
    npair = SB_HEADS // 2
    qcol = (A_QKV + DN_WIDTH) // SB_WIDTH
    return pl.pallas_call(
        _sb_kernel,
        grid=(batch, nq // SB_QB),
        in_specs=[
            pl.BlockSpec((seq, SB_WIDTH), lambda b, i: (b, qcol)),
            pl.BlockSpec((seq, SB_WIDTH), lambda b, i: (b, qcol + 1)),
            pl.BlockSpec((seq, SB_WIDTH), lambda b, i: (b, qcol + 2)),
            pl.BlockSpec((1, LANES), lambda b, i: (0, 0)),
            pl.BlockSpec((1, LANES), lambda b, i: (0, 0)),
            pl.BlockSpec((2 * SB_BLOCK, 2 * SB_BLOCK), lambda b, i: (0, 0)),
            pl.BlockSpec((2 * LANES, LANES), lambda b, i: (0, 0)),
        ],
        out_specs=pl.BlockSpec((SB_QB * SB_BLOCK, SB_WIDTH), lambda b, i: (b * (nq // SB_QB) + i, 0)),
        out_shape=jax.ShapeDtypeStruct((batch * seq, SB_WIDTH), BF16),
        scratch_shapes=[
            pltpu.VMEM((npair, nq, 2 * SB_BLOCK, LANES), BF16),
            pltpu.VMEM((seq, SB_WIDTH), BF16),
            pltpu.VMEM((npair, nq, 2 * SB_BLOCK, LANES), BF16),
            pltpu.VMEM((npair, 2 * SB_BLOCK, SB_BLOCK), F32),
            pltpu.VMEM((SB_BLOCK, SB_WIDTH), F32),
            pltpu.SMEM((2,), jnp.int32),
        ],
        compiler_params=pltpu.CompilerParams(
            dimension_semantics=("arbitrary", "arbitrary"), vmem_limit_bytes=VMEM_LIMIT),
        name="stickbreak",
    )(proj, proj, proj, qg2, kg2, cum_mat, half_ones2)


def _merge_kernel(x_ref, gain_ref, oa_ref, ob_ref, zb_ref, wg_ref, bg_ref, wa_ref, wb_ref, wo_ref, out_ref):
    x = x_ref[...]
    xn = (x * lax.rsqrt(jnp.mean(x * x, axis=-1, keepdims=True) + EPS) * gain_ref[...]).astype(BF16)
    gates = jax.nn.sigmoid(jnp.dot(xn, wg_ref[...], preferred_element_type=F32) + bg_ref[...])
    zb = zb_ref[...].astype(F32)
    ob = (ob_ref[...].astype(F32) * _silu(zb)).astype(BF16)
    ya = jnp.dot(oa_ref[...], wa_ref[...], preferred_element_type=F32)
    yb = jnp.dot(ob, wb_ref[...], preferred_element_type=F32)
    merged = gates[:, :D_MODEL] * ya + gates[:, D_MODEL:] * yb
    out_ref[...] = x + jnp.dot(merged.astype(BF16), wo_ref[...], preferred_element_type=F32)


def _merge(x2, gain, o_a, o_b, proj, w_gate, b_gate, w_up_a, w_up_b, w_out):
    t = x2.shape[0]
    zb_col = (A_QKV + DN_WIDTH + 3 * SB_WIDTH) // SB_WIDTH
    const = lambda i: (0, 0)
    return pl.pallas_call(
        _merge_kernel,
        grid=(t // OUT_TM,),
        in_specs=[
            pl.BlockSpec((OUT_TM, D_MODEL), lambda i: (i, 0)),
            pl.BlockSpec((1, D_MODEL), const),
            pl.BlockSpec((OUT_TM, DN_WIDTH), lambda i: (i, 0)),
            pl.BlockSpec((OUT_TM, SB_WIDTH), lambda i: (i, 0)),
            pl.BlockSpec((OUT_TM, SB_WIDTH), lambda i: (i, zb_col)),
            pl.BlockSpec((D_MODEL, 2 * D_MODEL), const, pipeline_mode=pl.Buffered(1)),
            pl.BlockSpec((1, 2 * D_MODEL), const),
            pl.BlockSpec((DN_WIDTH, D_MODEL), const, pipeline_mode=pl.Buffered(1)),
            pl.BlockSpec((SB_WIDTH, D_MODEL), const, pipeline_mode=pl.Buffered(1)),
            pl.BlockSpec((D_MODEL, D_MODEL), const, pipeline_mode=pl.Buffered(1)),
        ],
        out_specs=pl.BlockSpec((OUT_TM, D_MODEL), lambda i: (i, 0)),
        out_shape=jax.ShapeDtypeStruct((t, D_MODEL), F32),
        compiler_params=pltpu.CompilerParams(
            dimension_semantics=("arbitrary",), vmem_limit_bytes=VMEM_LIMIT),
        name="merge",
    )(x2, gain, o_a, o_b, proj, w_gate, b_gate, w_up_a, w_up_b, w_out)


def _layer(x, norm_gain, w_in, b_gate, conv_w, a_log, dt_bias, dn_out_gain,
           sb_q_gain, sb_k_gain, w_up_a, w_up_b, w_out):
    batch, seq, _ = x.shape
    x2 = x.reshape(batch * seq, D_MODEL)

    w_main = jnp.concatenate([w_in[:, :BA_COL0], w_in[:, BA_COL0 + BA_COLS:GATE_COL0]], axis=1).astype(BF16)
    w_gate = w_in[:, GATE_COL0:].astype(BF16)
    w_ba = jnp.pad(w_in[:, BA_COL0:BA_COL0 + BA_COLS], ((0, 0), (0, LANES - BA_COLS)))
    wba_hi = w_ba.astype(BF16)
    wba_lo = (w_ba - wba_hi.astype(F32)).astype(BF16)
    par = jnp.zeros((8, LANES), F32)
    par = par.at[0, DN_HEADS:2 * DN_HEADS].set(a_log).at[1, DN_HEADS:2 * DN_HEADS].set(dt_bias)
    qg2 = jnp.concatenate([sb_q_gain, sb_q_gain])[None, :]
    kg2 = jnp.concatenate([sb_k_gain, sb_k_gain])[None, :]
    kk = jnp.arange(SB_BLOCK)
    cum1 = jnp.concatenate([(kk[:, None] > kk[None, :]).astype(BF16), jnp.ones((SB_BLOCK, SB_BLOCK), BF16)], axis=1)
    cum_mat = jnp.concatenate([cum1, cum1], axis=0)
    ln = jnp.arange(LANES)
    half1 = ((ln[:, None] // SB_DH) == (ln[None, :] // SB_DH)).astype(BF16)
    half_ones2 = jnp.concatenate([half1, half1], axis=0)
    tt = jnp.arange(PAIR)
    ltri1 = ((tt[:, None] // CHUNK == tt[None, :] // CHUNK) & (tt[:, None] >= tt[None, :])).astype(BF16)
    ltri3 = jnp.concatenate([ltri1, ltri1, ltri1], axis=1)

    proj, ba = _inproj(x2, norm_gain[None, :], w_main, wba_hi, wba_lo)
    o_a = _deltanet(proj, ba, conv_w, par, dn_out_gain[None, :], ltri3, batch, seq)
    o_b = _stick_breaking(proj, qg2, kg2, cum_mat, half_ones2, batch, seq)
    out = _merge(x2, norm_gain[None, :], o_a, o_b, proj, w_gate, b_gate[None, :], w_up_a.astype(BF16),
                 w_up_b.astype(BF16), w_out.astype(BF16))
    return out.reshape(batch, seq, D_MODEL)


def kernel(x, norm_gain, w_in, b_gate, conv_w, a_log, dt_bias, dn_out_gain, sb_q_gain, sb_k_gain,
           w_up_a, w_up_b, w_out):
    h = x
    for layer in range(norm_gain.shape[0]):
        h = _layer(h, norm_gain[layer], w_in[layer], b_gate[layer], conv_w[layer], a_log[layer],
                   dt_bias[layer], dn_out_gain[layer], sb_q_gain[layer], sb_k_gain[layer],
                   w_up_a[layer], w_up_b[layer], w_out[layer])
    return h
```

```python
import functools

import jax
import jax.numpy as jnp
from jax import lax
from jax.experimental import pallas as pl
from jax.experimental.pallas import tpu as pltpu

F32 = jnp.float32
BF16 = jnp.bfloat16

D_MODEL = 1024
CHUNK = 64
SB_BLOCK = 128
EPS = 1e-6
LOG2E = 1.4426950408889634
ZERO_WEIGHT_LOG2 = -151.0
SB_LEAD = 3
SB_QB = 16
SB_NARROW = 32
DN_HEADS = 4
DN_DK = 128
DN_DV = 128
DN_CONV = 4
SB_HEADS = 8
SB_DH = 64
DN_WIDTH = DN_HEADS * DN_DV
SB_WIDTH = SB_HEADS * SB_DH
A_QKV = 2 * DN_HEADS * DN_DK + DN_WIDTH
BA_COL0 = A_QKV + DN_WIDTH
BA_COLS = 2 * DN_HEADS
MAIN_WIDTH = A_QKV + DN_WIDTH + 3 * SB_WIDTH + SB_WIDTH
GATE_COL0 = BA_COL0 + BA_COLS + 4 * SB_WIDTH

LANES = 128
PAIR = 2 * CHUNK

IN_TM, IN_TN = 1024, 4096
DN_TS = 1024
DN_WAVE = 256
OUT_TM = 1024

VMEM_LIMIT = 48 * 1024 * 1024


def _split3(x):
    hi = x.astype(BF16)
    r1 = x - hi.astype(F32)
    mid = r1.astype(BF16)
    lo = (r1 - mid.astype(F32)).astype(BF16)
    return hi, mid, lo


def _split2(x):
    hi = x.astype(BF16)
    lo = (x - hi.astype(F32)).astype(BF16)
    return hi, lo


def _silu(x):
    h = 0.5 * x
    return h + h * jnp.tanh(h)


def _inproj_kernel(x_ref, gain_ref, w_ref, wba_hi_ref, wba_lo_ref, proj_ref, ba_ref, xn_ref):
    j = pl.program_id(1)

    @pl.when(j == 0)
    def _():
        half = x_ref.shape[0] // 2
        for r in range(2):
            rows = slice(r * half, (r + 1) * half)
            x = x_ref[rows, :]
            xn = x * lax.rsqrt(jnp.mean(x * x, axis=-1, keepdims=True) + EPS) * gain_ref[...]
            hi, lo = _split2(xn)
            xn_ref[rows, :] = hi
            ba_ref[rows, :] = (jnp.dot(hi, wba_hi_ref[...], preferred_element_type=F32)
                               + jnp.dot(lo, wba_hi_ref[...], preferred_element_type=F32)
                               + jnp.dot(hi, wba_lo_ref[...], preferred_element_type=F32))
            proj_ref[rows, :] = jnp.dot(hi, w_ref[...], preferred_element_type=F32).astype(proj_ref.dtype)

    @pl.when(j != 0)
    def _():
        proj_ref[...] = jnp.dot(xn_ref[...], w_ref[...], preferred_element_type=F32).astype(proj_ref.dtype)


def _inproj(x2, gain, w_main, wba_hi, wba_lo):
    t = x2.shape[0]
    return pl.pallas_call(
        _inproj_kernel,
        grid=(t // IN_TM, MAIN_WIDTH // IN_TN),
        in_specs=[
            pl.BlockSpec((IN_TM, D_MODEL), lambda i, j: (i, 0)),
            pl.BlockSpec((1, D_MODEL), lambda i, j: (0, 0)),
            pl.BlockSpec((D_MODEL, IN_TN), lambda i, j: (0, j), pipeline_mode=pl.Buffered(1)),
            pl.BlockSpec((D_MODEL, LANES), lambda i, j: (0, 0)),
            pl.BlockSpec((D_MODEL, LANES), lambda i, j: (0, 0)),
        ],
        out_specs=[
            pl.BlockSpec((IN_TM, IN_TN), lambda i, j: (i, j)),
            pl.BlockSpec((IN_TM, LANES), lambda i, j: (i, 0)),
        ],
        out_shape=[
            jax.ShapeDtypeStruct((t, MAIN_WIDTH), BF16),
            jax.ShapeDtypeStruct((t, LANES), F32),
        ],
        scratch_shapes=[pltpu.VMEM((IN_TM, D_MODEL), BF16)],
        compiler_params=pltpu.CompilerParams(
            dimension_semantics=("arbitrary", "arbitrary"), vmem_limit_bytes=VMEM_LIMIT),
        name="inproj",
    )(x2, gain, w_main, wba_hi, wba_lo)


def _tri_inverse(ms, row, col):
    blk8 = (row >> 3) == (col >> 3)
    eye = jnp.where(row == col, 1.0, 0.0)
    pds = [jnp.where(blk8, -m, 0.0) for m in ms]
    pdb = [pd.astype(BF16) for pd in pds]
    xs = [eye + pd for pd in pds]
    p2b = [jnp.dot(b, b, preferred_element_type=F32).astype(BF16) for b in pdb]
    yield
    xs = [x + jnp.dot(x.astype(BF16), p2, preferred_element_type=F32) for x, p2 in zip(xs, p2b)]
    yield
    p4b = [jnp.dot(b, b, preferred_element_type=F32).astype(BF16) for b in p2b]
    yield
    xs = [x + jnp.dot(x.astype(BF16), p4, preferred_element_type=F32) for x, p4 in zip(xs, p4b)]
    yield
    for s in (8, 16, 32, 64):
        sh = s.bit_length()
        off = ((row >> sh) == (col >> sh)) & ((row & (2 * s - 1)) >= s) & ((col & (2 * s - 1)) < s)
        xb = [x.astype(BF16) for x in xs]
        ts_ = [jnp.dot(b, jnp.where(off, m, 0.0).astype(BF16), preferred_element_type=F32).astype(BF16)
               for b, m in zip(xb, ms)]
        yield
        xs = [x - jnp.dot(t, b, preferred_element_type=F32) for x, t, b in zip(xs, ts_, xb)]
        yield
    return xs


def _round_robin(*gens):
    live = list(gens)
    while live:
        for g in list(live):
            try:
                next(g)
            except StopIteration:
                live.remove(g)


def _deltanet_kernel(qkv_ref, z_ref, ba_ref, convw_ref, par_ref, gain_ref, ltri_ref, o_ref,
                     buf_ref, act_ref, state_ref, ol_ref, qt_ref, mc_ref, bc_ref, ge_ref):
    ts = qkv_ref.shape[0]
    nqk = DN_HEADS * DN_DK
    heads = range(DN_HEADS)
    nslab = A_QKV // LANES
    nwave = ts // DN_WAVE

    @pl.when(pl.program_id(1) == 0)
    def _():
        buf_ref[:, 0:8, :] = jnp.zeros((nslab, 8, LANES), F32)
        state_ref[...] = jnp.zeros_like(state_ref)

    for s in range(nslab):
        buf_ref[s, 8:8 + ts, :] = qkv_ref[:, s * LANES:(s + 1) * LANES].astype(F32)

    def activations(w):
        for r0 in range(w * DN_WAVE, (w + 1) * DN_WAVE, PAIR):
            for s in range(nslab):
                cs = slice(s * LANES, (s + 1) * LANES)
                acc = None
                for i in range(DN_CONV):
                    lo = 8 - (DN_CONV - 1) + i + r0
                    term = buf_ref[s, lo:lo + PAIR, :] * convw_ref[i:i + 1, cs]
                    acc = term if acc is None else acc + term
                y = _silu(acc)
                if s < 2 * DN_HEADS:
                    y = y * lax.rsqrt(jnp.sum(y * y, axis=-1, keepdims=True) + EPS)
                if s < DN_HEADS:
                    y = y * (DN_DK ** -0.5)
                act_ref[r0:r0 + PAIR, cs] = y
                yield

    ba = ba_ref[...]
    beta_all = jax.nn.sigmoid(ba)
    xs = ba + par_ref[1:2, :]
    softplus = jnp.maximum(xs, 0.0) + jnp.log(1.0 + jnp.exp(-jnp.abs(xs)))
    g_all = -jnp.exp(par_ref[0:1, :]) * softplus
    gc_of = [jnp.dot(ltri_ref[...], jnp.concatenate(_split3(g_all[p * PAIR:(p + 1) * PAIR, :]), axis=0),
                     preferred_element_type=F32) for p in range(ts // PAIR)]

    row = lax.broadcasted_iota(jnp.int32, (PAIR, PAIR), 0)
    col = lax.broadcasted_iota(jnp.int32, (PAIR, PAIR), 1)
    lower = row >= col
    strict = row > col

    def chunk_local(w):
        pairs = range(w * (DN_WAVE // PAIR), (w + 1) * (DN_WAVE // PAIR))
        units = [(p, h) for p in pairs for h in heads]
        rows_of = [slice(p * PAIR, (p + 1) * PAIR) for p, _ in units]
        gc_t = {p: gc_of[p].T for p in pairs}
        q = [act_ref[rs, h * DN_DK:(h + 1) * DN_DK] for rs, (_, h) in zip(rows_of, units)]
        k = [act_ref[rs, nqk + h * DN_DK:nqk + (h + 1) * DN_DK] for rs, (_, h) in zip(rows_of, units)]
        v = [act_ref[rs, 2 * nqk + h * DN_DV:2 * nqk + (h + 1) * DN_DV] for rs, (_, h) in zip(rows_of, units)]
        beta = [beta_all[rs, h:h + 1] for rs, (_, h) in zip(rows_of, units)]
        gcc = [gc_of[p][:, DN_HEADS + h:DN_HEADS + h + 1] for p, h in units]
        decay = [jnp.where(lower, jnp.exp(g - gc_t[p][DN_HEADS + h:DN_HEADS + h + 1, :]), 0.0)
                 for g, (p, h) in zip(gcc, units)]
        eg = [jnp.exp(g) for g in gcc]
        gend = [g[PAIR - 1:PAIR, :] for g in gcc]
        kb = [x.astype(BF16) for x in k]
        yield
        qkk = [lax.dot_general(jnp.concatenate([qq.astype(BF16), kk], axis=0), kk,
                               (((1,), (1,)), ((), ())), preferred_element_type=F32) for qq, kk in zip(q, kb)]
        yield
        ms = [jnp.where(strict, b * x[PAIR:] * d, 0.0) for b, x, d in zip(beta, qkk, decay)]
        tinv = yield from _tri_inverse(ms, row, col)
        sol = [jnp.dot(t.astype(BF16),
                       jnp.concatenate([(b * vv).astype(BF16), (b * e * kk).astype(BF16)], axis=1),
                       preferred_element_type=F32) for t, b, vv, e, kk in zip(tinv, beta, v, eg, k)]
        yield
        solb = [x.astype(BF16) for x in sol]
        qkb = [jnp.where(lower, x[:PAIR] * d, 0.0).astype(BF16) for x, d in zip(qkk, decay)]
        qwu = [jnp.dot(a, b, preferred_element_type=F32) for a, b in zip(qkb, solb)]
        yield
        kd = [kk * jnp.exp(ge - g) for kk, ge, g in zip(k, gend, gcc)]
        kwu = [lax.dot_general(a.astype(BF16), b, (((0,), (0,)), ((), ())), preferred_element_type=F32)
               for a, b in zip(kd, solb)]
        yield
        for (p, h), x, g in zip(units, kwu, gcc):
            bc_ref[h, p] = x[:, :DN_DV]
            mc_ref[h, p] = x[:, DN_DV:].astype(BF16)
            ge_ref[h, p] = jnp.broadcast_to(jnp.exp(g[PAIR - 1:PAIR, :]), (8, LANES))
        for (p, h), rs, x, qq, e in zip(units, rows_of, qwu, q, eg):
            ol_ref[h, rs, :] = x[:, :DN_DV]
            qt_ref[h, rs, :] = (qq * e - x[:, DN_DV:]).astype(BF16)
        yield

    gain = gain_ref[...]
    st = [state_ref[h] for h in heads]

    def recurrence(w):
        for cc in range(w * (DN_WAVE // PAIR), (w + 1) * (DN_WAVE // PAIR)):
            cr = slice(cc * PAIR, (cc + 1) * PAIR)
            sb = [x.astype(BF16) for x in st]
            o = [ol_ref[h, cr, :] + jnp.dot(qt_ref[h, cr, :], sb[h], preferred_element_type=F32) for h in heads]
            new = [st[h] * ge_ref[h, cc, 0:1, :] + bc_ref[h, cc]
                   - jnp.dot(mc_ref[h, cc], sb[h], preferred_element_type=F32) for h in heads]
            for h in heads:
                st[h] = new[h]
            yield
            for h in heads:
                hs = slice(h * DN_DV, (h + 1) * DN_DV)
                on = o[h] * lax.rsqrt(jnp.mean(o[h] * o[h], axis=-1, keepdims=True) + EPS) * gain
                zz = z_ref[cr, hs].astype(F32)
                o_ref[cr, hs] = (on * _silu(zz)).astype(o_ref.dtype)
            yield

    _round_robin(activations(0))
    for w in range(nwave):
        others = []
        if w + 1 < nwave:
            others.append(activations(w + 1))
        if w > 0:
            others.append(recurrence(w - 1))
        _round_robin(chunk_local(w), *others)
    _round_robin(recurrence(nwave - 1))
    for h in heads:
        state_ref[h] = st[h]
    for s in range(nslab):
        buf_ref[s, 0:8, :] = buf_ref[s, ts:ts + 8, :]


def _deltanet(proj, ba, conv_w, par, gain, ltri3, batch, seq):
    nsb = seq // DN_TS
    nchunk = DN_TS // PAIR
    return pl.pallas_call(
        _deltanet_kernel,
        grid=(batch, nsb),
        in_specs=[
            pl.BlockSpec((DN_TS, A_QKV), lambda b, s: (b * nsb + s, 0)),
            pl.BlockSpec((DN_TS, DN_WIDTH), lambda b, s: (b * nsb + s, A_QKV // DN_WIDTH)),
            pl.BlockSpec((DN_TS, LANES), lambda b, s: (b * nsb + s, 0)),
            pl.BlockSpec((DN_CONV, A_QKV), lambda b, s: (0, 0)),
            pl.BlockSpec((8, LANES), lambda b, s: (0, 0)),
            pl.BlockSpec((1, DN_DV), lambda b, s: (0, 0)),
            pl.BlockSpec((PAIR, 3 * PAIR), lambda b, s: (0, 0)),
        ],
        out_specs=pl.BlockSpec((DN_TS, DN_WIDTH), lambda b, s: (b * nsb + s, 0)),
        out_shape=jax.ShapeDtypeStruct((batch * seq, DN_WIDTH), BF16),
        scratch_shapes=[
            pltpu.VMEM((A_QKV // LANES, DN_TS + 8, LANES), F32),
            pltpu.VMEM((DN_TS, A_QKV), F32),
            pltpu.VMEM((DN_HEADS, DN_DK, DN_DV), F32),
            pltpu.VMEM((DN_HEADS, DN_TS, DN_DV), F32),
            pltpu.VMEM((DN_HEADS, DN_TS, DN_DK), BF16),
            pltpu.VMEM((DN_HEADS, nchunk, DN_DK, DN_DK), BF16),
            pltpu.VMEM((DN_HEADS, nchunk, DN_DK, DN_DV), F32),
            pltpu.VMEM((DN_HEADS, nchunk, 8, LANES), F32),
        ],
        compiler_params=pltpu.CompilerParams(
            dimension_semantics=("arbitrary", "arbitrary"), vmem_limit_bytes=VMEM_LIMIT),
        name="deltanet",
    )(proj, proj, ba, conv_w, par, gain, ltri3)


def _rms_halves(x, half_ones2, gain):
    hi, lo = _split2(x * x)
    ss = jnp.dot(jnp.concatenate([hi, lo], axis=1), half_ones2, preferred_element_type=F32)
    return x * lax.rsqrt(ss + SB_DH * EPS) * (gain * SB_DH ** 0.5)


def _sb_block(i, k_ref, cum_ref, o_ref, q2_ref, kn_ref, v2_ref, r_ref, acc_ref, flag_ref):
    npair = k_ref.shape[1] // LANES
    pairs = range(npair)

    def cols(p):
        return slice(p * LANES, (p + 1) * LANES)

    q2 = [q2_ref[p, i] for p in pairs]

    def scores(rows, mask):
        zs = [lax.dot_general(q2[p], kn_ref[rows, cols(p)], (((1,), (1,)), ((), ())),
                              preferred_element_type=F32) for p in pairs]
        lss = [jnp.minimum(z, 0.0) - jnp.log(1.0 + jnp.exp2(-jnp.abs(z))) * LOG2E for z in zs]
        lks = [ls - z for ls, z in zip(lss, zs)]
        if mask is not None:
            lks = [jnp.where(mask, lk, 0.0) for lk in lks]
        return lss, [jnp.concatenate(_split2(lk), axis=1) for lk in lks]

    def cumsums(lk2s):
        return [jnp.dot(lk2, cum_ref[...], preferred_element_type=F32) for lk2 in lk2s]

    def weights(lss, css, rs, mask):
        if rs is None:
            avs = [jnp.exp2(ls + cs[:, :SB_BLOCK]) for ls, cs in zip(lss, css)]
        else:
            avs = [jnp.exp2(ls + cs[:, :SB_BLOCK] + r) for ls, cs, r in zip(lss, css, rs)]
        if mask is not None:
            avs = [jnp.where(mask, a, 0.0) for a in avs]
        out = []
        for a in avs:
            ab = a.astype(BF16)
            out.append(jnp.concatenate([ab[:SB_BLOCK], ab[SB_BLOCK:]], axis=1))
        return out

    def tile_rows(j):
        return pl.ds(pl.multiple_of(j * SB_BLOCK, SB_BLOCK), SB_BLOCK)

    def sweep(tiles, masks, rs, accs):
        sc = [scores(tile_rows(j), m) for j, m in zip(tiles, masks)]
        css = [cumsums(lk2s) for _, lk2s in sc]
        for j, m, (lss, _), cs in zip(tiles, masks, sc, css):
            a2s = weights(lss, cs, rs, m)
            rs = [c[:, SB_BLOCK:] for c in cs] if rs is None else [r + c[:, SB_BLOCK:] for r, c in zip(rs, cs)]
            accs = [acc + jnp.dot(a2, v2_ref[p, j], preferred_element_type=F32)
                    for p, (acc, a2) in enumerate(zip(accs, a2s))]
        return rs, accs

    def load_state():
        return [r_ref[p] for p in pairs], [acc_ref[:, cols(p)] for p in pairs]

    def store_state(rs, accs):
        for p in pairs:
            r_ref[p] = rs[p]
            acc_ref[:, cols(p)] = accs[p]

    qrow = lax.broadcasted_iota(jnp.int32, (2 * SB_BLOCK, SB_BLOCK), 0) & (SB_BLOCK - 1)
    kcol = lax.broadcasted_iota(jnp.int32, (2 * SB_BLOCK, SB_BLOCK), 1)
    causal = kcol < qrow
    zero_accs = [jnp.zeros((SB_BLOCK, LANES), F32)] * npair

    def alive_flag(rs):
        return (jnp.max(functools.reduce(jnp.maximum, rs)) >= ZERO_WEIGHT_LOG2).astype(jnp.int32)

    def head_rows(x, lo, hi):
        return jnp.concatenate([x[lo:hi], x[SB_BLOCK + lo:SB_BLOCK + hi]], axis=0)

    @pl.when(i == 0)
    def _():
        rs, accs = sweep([i], [causal], None, zero_accs)
        store_state(rs, accs)
        flag_ref[0] = alive_flag(rs)
        flag_ref[1] = jnp.int32(0)

    @pl.when(i == 1)
    def _():
        rs, accs = sweep([i, i - 1], [causal, None], None, zero_accs)
        store_state(rs, accs)
        flag_ref[0] = alive_flag(rs)
        flag_ref[1] = jnp.int32(0)

    @pl.when(i >= 2)
    def _():
        j3 = i - 2
        sc = [scores(tile_rows(j), m) for j, m in ((i, causal), (i - 1, None))]
        q2n = [head_rows(q, 0, SB_NARROW) for q in q2]
        zs = [lax.dot_general(q2n[p], kn_ref[tile_rows(j3), cols(p)], (((1,), (1,)), ((), ())),
                              preferred_element_type=F32) for p in pairs]
        lss_n = [jnp.minimum(z, 0.0) - jnp.log(1.0 + jnp.exp2(-jnp.abs(z))) * LOG2E for z in zs]
        lk2_n = [jnp.concatenate(_split2(ls - z), axis=1) for ls, z in zip(lss_n, zs)]
        css = [cumsums(lk2s) for _, lk2s in sc]
        css_n = cumsums(lk2_n)
        rs, accs = None, zero_accs
        for j, m, (lss, _), cs in zip((i, i - 1), (causal, None), sc, css):
            a2s = weights(lss, cs, rs, m)
            rs = [c[:, SB_BLOCK:] for c in cs] if rs is None else [r + c[:, SB_BLOCK:] for r, c in zip(rs, cs)]
            accs = [acc + jnp.dot(a2, v2_ref[p, j], preferred_element_type=F32)
                    for p, (acc, a2) in enumerate(zip(accs, a2s))]
        rest = [head_rows(r, SB_NARROW, SB_BLOCK) for r in rs]
        rs_n = [head_rows(r, 0, SB_NARROW) for r in rs]
        a_n = [jnp.exp2(ls + cs[:, :SB_BLOCK] + r).astype(BF16) for ls, cs, r in zip(lss_n, css_n, rs_n)]
        top = [accs[p][0:SB_NARROW]
               + jnp.dot(jnp.concatenate([a_n[p][:SB_NARROW], a_n[p][SB_NARROW:]], axis=1), v2_ref[p, j3],
                         preferred_element_type=F32) for p in pairs]
        rs_n = [r + cs[:, SB_BLOCK:] for r, cs in zip(rs_n, css_n)]
        store_state(rs, accs)
        for p in pairs:
            r_ref[p, 0:SB_NARROW, :] = rs_n[p][:SB_NARROW]
            r_ref[p, SB_BLOCK:SB_BLOCK + SB_NARROW, :] = rs_n[p][SB_NARROW:]
            acc_ref[0:SB_NARROW, cols(p)] = top[p]
        flag_ref[1] = alive_flag(rest)
        flag_ref[0] = jnp.maximum(alive_flag(rest), alive_flag(rs_n))

    @pl.when(flag_ref[1] > 0)
    def _():
        rs, accs = sweep([i - 2], [qrow >= SB_NARROW], *load_state())
        store_state(rs, accs)
        flag_ref[0] = alive_flag(rs)

    def cond(carry):
        t, alive = carry
        return jnp.logical_and(t < i, alive > 0)

    def body(carry):
        t, _ = carry
        j = i - 1 - t
        rs, accs = sweep([j], [None], *load_state())
        store_state(rs, accs)
        return t + 1, alive_flag(rs)
    lax.while_loop(cond, body, (jnp.int32(SB_LEAD - 1), flag_ref[0]))

    o_ref[...] = acc_ref[...].astype(o_ref.dtype)


def _sb_kernel(q_ref, k_ref, v_ref, qg_ref, kg_ref, cum_ref, half_ref, o_ref,
               q2_ref, kn_ref, v2_ref, r_ref, acc_ref, flag_ref):
    step = pl.program_id(1)
    nblk = k_ref.shape[0] // SB_BLOCK
    pairs = range(k_ref.shape[1] // LANES)
    first = lax.broadcasted_iota(jnp.int32, (SB_BLOCK, LANES), 1) < SB_DH

    def cols(p):
        return slice(p * LANES, (p + 1) * LANES)

    @pl.when(step == 0)
    def _():
        qscale = qg_ref[...] * (SB_DH ** -0.5 * LOG2E)

        def prep(j, carry):
            rows = pl.ds(pl.multiple_of(j * SB_BLOCK, SB_BLOCK), SB_BLOCK)
            for p in pairs:
                qn = _rms_halves(q_ref[rows, cols(p)].astype(F32), half_ref[...], qscale).astype(BF16)
                zero = jnp.zeros_like(qn)
                q2_ref[p, j, 0:SB_BLOCK, :] = jnp.where(first, qn, zero)
                q2_ref[p, j, SB_BLOCK:2 * SB_BLOCK, :] = jnp.where(first, zero, qn)
                kn = _rms_halves(k_ref[rows, cols(p)].astype(F32), half_ref[...], kg_ref[...])
                kn_ref[rows, cols(p)] = kn.astype(BF16)
                vb = v_ref[rows, cols(p)]
                v2_ref[p, j, 0:SB_BLOCK, :] = jnp.where(first, vb, zero)
                v2_ref[p, j, SB_BLOCK:2 * SB_BLOCK, :] = jnp.where(first, zero, vb)
            return carry
        lax.fori_loop(0, nblk, prep, 0)

    def block(sub, carry):
        out_rows = pl.ds(pl.multiple_of(sub * SB_BLOCK, SB_BLOCK), SB_BLOCK)
        _sb_block(step * SB_QB + sub, k_ref, cum_ref, o_ref.at[out_rows, :],
                  q2_ref, kn_ref, v2_ref, r_ref, acc_ref, flag_ref)
        return carry
    lax.fori_loop(0, SB_QB, block, 0)


def _stick_breaking(proj, qg2, kg2, cum_mat, half_ones2, batch, seq):
    nq = seq // SB_BLOCK
    npair = SB_HEADS // 2
    qcol = (A_QKV + DN_WIDTH) // SB_WIDTH
    return pl.pallas_call(
        _sb_kernel,
        grid=(batch, nq // SB_QB),
        in_specs=[
            pl.BlockSpec((seq, SB_WIDTH), lambda b, i: (b, qcol)),
            pl.BlockSpec((seq, SB_WIDTH), lambda b, i: (b, qcol + 1)),
            pl.BlockSpec((seq, SB_WIDTH), lambda b, i: (b, qcol + 2)),
            pl.BlockSpec((1, LANES), lambda b, i: (0, 0)),
            pl.BlockSpec((1, LANES), lambda b, i: (0, 0)),
            pl.BlockSpec((2 * SB_BLOCK, 2 * SB_BLOCK), lambda b, i: (0, 0)),
            pl.BlockSpec((2 * LANES, LANES), lambda b, i: (0, 0)),
        ],
        out_specs=pl.BlockSpec((SB_QB * SB_BLOCK, SB_WIDTH), lambda b, i: (b * (nq // SB_QB) + i, 0)),
        out_shape=jax.ShapeDtypeStruct((batch * seq, SB_WIDTH), BF16),
        scratch_shapes=[
            pltpu.VMEM((npair, nq, 2 * SB_BLOCK, LANES), BF16),
            pltpu.VMEM((seq, SB_WIDTH), BF16),
            pltpu.VMEM((npair, nq, 2 * SB_BLOCK, LANES), BF16),
            pltpu.VMEM((npair, 2 * SB_BLOCK, SB_BLOCK), F32),
            pltpu.VMEM((SB_BLOCK, SB_WIDTH), F32),
            pltpu.SMEM((2,), jnp.int32),
        ],
        compiler_params=pltpu.CompilerParams(
            dimension_semantics=("arbitrary", "arbitrary"), vmem_limit_bytes=VMEM_LIMIT),
        name="stickbreak",
    )(proj, proj, proj, qg2, kg2, cum_mat, half_ones2)


def _merge_kernel(x_ref, gain_ref, oa_ref, ob_ref, zb_ref, wg_ref, bg_ref, wa_ref, wb_ref, wo_ref, out_ref):
    x = x_ref[...]
    xn = (x * lax.rsqrt(jnp.mean(x * x, axis=-1, keepdims=True) + EPS) * gain_ref[...]).astype(BF16)
    gates = jax.nn.sigmoid(jnp.dot(xn, wg_ref[...], preferred_element_type=F32) + bg_ref[...])
    zb = zb_ref[...].astype(F32)
    ob = (ob_ref[...].astype(F32) * _silu(zb)).astype(BF16)
    ya = jnp.dot(oa_ref[...], wa_ref[...], preferred_element_type=F32)
    yb = jnp.dot(ob, wb_ref[...], preferred_element_type=F32)
    merged = gates[:, :D_MODEL] * ya + gates[:, D_MODEL:] * yb
    out_ref[...] = x + jnp.dot(merged.astype(BF16), wo_ref[...], preferred_element_type=F32)


def _merge(x2, gain, o_a, o_b, proj, w_gate, b_gate, w_up_a, w_up_b, w_out):
    t = x2.shape[0]
    zb_col = (A_QKV + DN_WIDTH + 3 * SB_WIDTH) // SB_WIDTH
    const = lambda i: (0, 0)
    return pl.pallas_call(
        _merge_kernel,
        grid=(t // OUT_TM,),
        in_specs=[
            pl.BlockSpec((OUT_TM, D_MODEL), lambda i: (i, 0)),
            pl.BlockSpec((1, D_MODEL), const),
            pl.BlockSpec((OUT_TM, DN_WIDTH), lambda i: (i, 0)),
            pl.BlockSpec((OUT_TM, SB_WIDTH), lambda i: (i, 0)),
            pl.BlockSpec((OUT_TM, SB_WIDTH), lambda i: (i, zb_col)),
            pl.BlockSpec((D_MODEL, 2 * D_MODEL), const, pipeline_mode=pl.Buffered(1)),
            pl.BlockSpec((1, 2 * D_MODEL), const),
            pl.BlockSpec((DN_WIDTH, D_MODEL), const, pipeline_mode=pl.Buffered(1)),
            pl.BlockSpec((SB_WIDTH, D_MODEL), const, pipeline_mode=pl.Buffered(1)),
            pl.BlockSpec((D_MODEL, D_MODEL), const, pipeline_mode=pl.Buffered(1)),
        ],
        out_specs=pl.BlockSpec((OUT_TM, D_MODEL), lambda i: (i, 0)),
        out_shape=jax.ShapeDtypeStruct((t, D_MODEL), F32),
        compiler_params=pltpu.CompilerParams(
            dimension_semantics=("arbitrary",), vmem_limit_bytes=VMEM_LIMIT),
        name="merge",
    )(x2, gain, o_a, o_b, proj, w_gate, b_gate, w_up_a, w_up_b, w_out)


def _layer(x, norm_gain, w_in, b_gate, conv_w, a_log, dt_bias, dn_out_gain,
           sb_q_gain, sb_k_gain, w_up_a, w_up_b, w_out):
    batch, seq, _ = x.shape
    x2 = x.reshape(batch * seq, D_MODEL)

    w_main = jnp.concatenate([w_in[:, :BA_COL0], w_in[:, BA_COL0 + BA_COLS:GATE_COL0]], axis=1).astype(BF16)
    w_gate = w_in[:, GATE_COL0:].astype(BF16)
    w_ba = jnp.pad(w_in[:, BA_COL0:BA_COL0 + BA_COLS], ((0, 0), (0, LANES - BA_COLS)))
    wba_hi = w_ba.astype(BF16)
    wba_lo = (w_ba - wba_hi.astype(F32)).astype(BF16)
    par = jnp.zeros((8, LANES), F32)
    par = par.at[0, DN_HEADS:2 * DN_HEADS].set(a_log).at[1, DN_HEADS:2 * DN_HEADS].set(dt_bias)
    qg2 = jnp.concatenate([sb_q_gain, sb_q_gain])[None, :]
    kg2 = jnp.concatenate([sb_k_gain, sb_k_gain])[None, :]
    kk = jnp.arange(SB_BLOCK)
    cum1 = jnp.concatenate([(kk[:, None] > kk[None, :]).astype(BF16), jnp.ones((SB_BLOCK, SB_BLOCK), BF16)], axis=1)
    cum_mat = jnp.concatenate([cum1, cum1], axis=0)
    ln = jnp.arange(LANES)
    half1 = ((ln[:, None] // SB_DH) == (ln[None, :] // SB_DH)).astype(BF16)
    half_ones2 = jnp.concatenate([half1, half1], axis=0)
    tt = jnp.arange(PAIR)
    ltri1 = (tt[:, None] >= tt[None, :]).astype(BF16)
    ltri3 = jnp.concatenate([ltri1, ltri1, ltri1], axis=1)

    proj, ba = _inproj(x2, norm_gain[None, :], w_main, wba_hi, wba_lo)
    o_a = _deltanet(proj, ba, conv_w, par, dn_out_gain[None, :], ltri3, batch, seq)
    o_b = _stick_breaking(proj, qg2, kg2, cum_mat, half_ones2, batch, seq)
    out = _merge(x2, norm_gain[None, :], o_a, o_b, proj, w_gate, b_gate[None, :], w_up_a.astype(BF16),
                 w_up_b.astype(BF16), w_out.astype(BF16))
    return out.reshape(batch, seq, D_MODEL)


def kernel(x, norm_gain, w_in, b_gate, conv_w, a_log, dt_bias, dn_out_gain, sb_q_gain, sb_k_gain,
           w_up_a, w_up_b, w_out):
    h = x
    for layer in range(norm_gain.shape[0]):
        h = _layer(h, norm_gain[layer], w_in[layer], b_gate[layer], conv_w[layer], a_log[layer],
                   dt_bias[layer], dn_out_gain[layer], sb_q_gain[layer], sb_k_gain[layer],
                   w_up_a[layer], w_up_b[layer], w_out[layer])
    return h
```

```python
import functools

import jax
import jax.numpy as jnp
from jax import lax
from jax.experimental import pallas as pl
from jax.experimental.pallas import tpu as pltpu

F32 = jnp.float32
BF16 = jnp.bfloat16

D_MODEL = 1024
CHUNK = 64
SB_BLOCK = 128
EPS = 1e-6
LOG2E = 1.4426950408889634
ZERO_WEIGHT_LOG2 = -151.0
SB_LEAD = 3
SB_QB = 16
SB_NARROW = 32
DN_HEADS = 4
DN_DK = 128
DN_DV = 128
DN_CONV = 4
SB_HEADS = 8
SB_DH = 64
DN_WIDTH = DN_HEADS * DN_DV
SB_WIDTH = SB_HEADS * SB_DH
A_QKV = 2 * DN_HEADS * DN_DK + DN_WIDTH
BA_COL0 = A_QKV + DN_WIDTH
BA_COLS = 2 * DN_HEADS
MAIN_WIDTH = A_QKV + DN_WIDTH + 3 * SB_WIDTH + SB_WIDTH
GATE_COL0 = BA_COL0 + BA_COLS + 4 * SB_WIDTH

LANES = 128
PAIR = 2 * CHUNK

IN_TM, IN_TN = 1024, 4096
DN_TS = 1024
DN_WAVE = 256
OUT_TM = 1024

VMEM_LIMIT = 48 * 1024 * 1024


def _split3(x):
    hi = x.astype(BF16)
    r1 = x - hi.astype(F32)
    mid = r1.astype(BF16)
    lo = (r1 - mid.astype(F32)).astype(BF16)
    return hi, mid, lo


def _split2(x):
    hi = x.astype(BF16)
    lo = (x - hi.astype(F32)).astype(BF16)
    return hi, lo


def _silu(x):
    h = 0.5 * x
    return h + h * jnp.tanh(h)


def _inproj_kernel(x_ref, gain_ref, w_ref, wba_hi_ref, wba_lo_ref, proj_ref, ba_ref, xn_ref):
    j = pl.program_id(1)

    @pl.when(j == 0)
    def _():
        half = x_ref.shape[0] // 2
        for r in range(2):
            rows = slice(r * half, (r + 1) * half)
            x = x_ref[rows, :]
            xn = x * lax.rsqrt(jnp.mean(x * x, axis=-1, keepdims=True) + EPS) * gain_ref[...]
            hi, lo = _split2(xn)
            xn_ref[rows, :] = hi
            ba_ref[rows, :] = jnp.dot(
                jnp.concatenate([hi, lo, hi], axis=1),
                jnp.concatenate([wba_hi_ref[...], wba_hi_ref[...], wba_lo_ref[...]], axis=0),
                preferred_element_type=F32)
            proj_ref[rows, :] = jnp.dot(hi, w_ref[...], preferred_element_type=F32).astype(proj_ref.dtype)

    @pl.when(j != 0)
    def _():
        proj_ref[...] = jnp.dot(xn_ref[...], w_ref[...], preferred_element_type=F32).astype(proj_ref.dtype)


def _inproj(x2, gain, w_main, wba_hi, wba_lo):
    t = x2.shape[0]
    return pl.pallas_call(
        _inproj_kernel,
        grid=(t // IN_TM, MAIN_WIDTH // IN_TN),
        in_specs=[
            pl.BlockSpec((IN_TM, D_MODEL), lambda i, j: (i, 0)),
            pl.BlockSpec((1, D_MODEL), lambda i, j: (0, 0)),
            pl.BlockSpec((D_MODEL, IN_TN), lambda i, j: (0, j), pipeline_mode=pl.Buffered(1)),
            pl.BlockSpec((D_MODEL, LANES), lambda i, j: (0, 0)),
            pl.BlockSpec((D_MODEL, LANES), lambda i, j: (0, 0)),
        ],
        out_specs=[
            pl.BlockSpec((IN_TM, IN_TN), lambda i, j: (i, j)),
            pl.BlockSpec((IN_TM, LANES), lambda i, j: (i, 0)),
        ],
        out_shape=[
            jax.ShapeDtypeStruct((t, MAIN_WIDTH), BF16),
            jax.ShapeDtypeStruct((t, LANES), F32),
        ],
        scratch_shapes=[pltpu.VMEM((IN_TM, D_MODEL), BF16)],
        compiler_params=pltpu.CompilerParams(
            dimension_semantics=("arbitrary", "arbitrary"), vmem_limit_bytes=VMEM_LIMIT),
        name="inproj",
    )(x2, gain, w_main, wba_hi, wba_lo)


def _tri_inverse(ms, row, col):
    blk8 = (row >> 3) == (col >> 3)
    eye = jnp.where(row == col, 1.0, 0.0)
    pds = [jnp.where(blk8, -m, 0.0) for m in ms]
    pdb = [pd.astype(BF16) for pd in pds]
    xs = [eye + pd for pd in pds]
    p2b = [jnp.dot(b, b, preferred_element_type=F32).astype(BF16) for b in pdb]
    yield
    xs = [x + jnp.dot(x.astype(BF16), p2, preferred_element_type=F32) for x, p2 in zip(xs, p2b)]
    yield
    p4b = [jnp.dot(b, b, preferred_element_type=F32).astype(BF16) for b in p2b]
    yield
    xs = [x + jnp.dot(x.astype(BF16), p4, preferred_element_type=F32) for x, p4 in zip(xs, p4b)]
    yield
    for s in (8, 16, 32):
        sh = s.bit_length()
        off = ((row >> sh) == (col >> sh)) & ((row & (2 * s - 1)) >= s) & ((col & (2 * s - 1)) < s)
        xb = [x.astype(BF16) for x in xs]
        ts_ = [jnp.dot(b, jnp.where(off, m, 0.0).astype(BF16), preferred_element_type=F32).astype(BF16)
               for b, m in zip(xb, ms)]
        yield
        xs = [x - jnp.dot(t, b, preferred_element_type=F32) for x, t, b in zip(xs, ts_, xb)]
        yield
    return xs


def _round_robin(*gens):
    live = list(gens)
    while live:
        for g in list(live):
            try:
                next(g)
            except StopIteration:
                live.remove(g)


def _deltanet_kernel(qkv_ref, z_ref, ba_ref, convw_ref, par_ref, gain_ref, ltri_ref, o_ref,
                     buf_ref, act_ref, state_ref, ol_ref, qt_ref, mc_ref, bc_ref, ge_ref):
    ts = qkv_ref.shape[0]
    nqk = DN_HEADS * DN_DK
    heads = range(DN_HEADS)
    nslab = A_QKV // LANES
    nwave = ts // DN_WAVE

    @pl.when(pl.program_id(1) == 0)
    def _():
        buf_ref[:, 0:8, :] = jnp.zeros((nslab, 8, LANES), F32)
        state_ref[...] = jnp.zeros_like(state_ref)

    for s in range(nslab):
        buf_ref[s, 8:8 + ts, :] = qkv_ref[:, s * LANES:(s + 1) * LANES].astype(F32)

    def activations(w):
        for r0 in range(w * DN_WAVE, (w + 1) * DN_WAVE, PAIR):
            for s in range(nslab):
                cs = slice(s * LANES, (s + 1) * LANES)
                acc = None
                for i in range(DN_CONV):
                    lo = 8 - (DN_CONV - 1) + i + r0
                    term = buf_ref[s, lo:lo + PAIR, :] * convw_ref[i:i + 1, cs]
                    acc = term if acc is None else acc + term
                y = _silu(acc)
                if s < 2 * DN_HEADS:
                    y = y * lax.rsqrt(jnp.sum(y * y, axis=-1, keepdims=True) + EPS)
                if s < DN_HEADS:
                    y = y * (DN_DK ** -0.5)
                act_ref[r0:r0 + PAIR, cs] = y
                yield

    ba = ba_ref[...]
    beta_all = jax.nn.sigmoid(ba)
    xs = ba + par_ref[1:2, :]
    softplus = jnp.maximum(xs, 0.0) + jnp.log(1.0 + jnp.exp(-jnp.abs(xs)))
    g_all = -jnp.exp(par_ref[0:1, :]) * softplus
    gc_of = [jnp.dot(ltri_ref[...], jnp.concatenate(_split3(g_all[p * PAIR:(p + 1) * PAIR, :]), axis=0),
                     preferred_element_type=F32) for p in range(ts // PAIR)]

    row = lax.broadcasted_iota(jnp.int32, (PAIR, PAIR), 0)
    col = lax.broadcasted_iota(jnp.int32, (PAIR, PAIR), 1)
    same = (row >> 6) == (col >> 6)
    lower = same & (row >= col)
    strict = same & (row > col)
    rcol = lax.broadcasted_iota(jnp.int32, (PAIR, 1), 0)

    def chunk_local(w):
        pairs = range(w * (DN_WAVE // PAIR), (w + 1) * (DN_WAVE // PAIR))
        units = [(p, h) for p in pairs for h in heads]
        rows_of = [slice(p * PAIR, (p + 1) * PAIR) for p, _ in units]
        gc_t = {p: gc_of[p].T for p in pairs}
        q = [act_ref[rs, h * DN_DK:(h + 1) * DN_DK] for rs, (_, h) in zip(rows_of, units)]
        k = [act_ref[rs, nqk + h * DN_DK:nqk + (h + 1) * DN_DK] for rs, (_, h) in zip(rows_of, units)]
        v = [act_ref[rs, 2 * nqk + h * DN_DV:2 * nqk + (h + 1) * DN_DV] for rs, (_, h) in zip(rows_of, units)]
        beta = [beta_all[rs, h:h + 1] for rs, (_, h) in zip(rows_of, units)]
        gcc = [gc_of[p][:, DN_HEADS + h:DN_HEADS + h + 1] for p, h in units]
        decay = [jnp.where(lower, jnp.exp(g - gc_t[p][DN_HEADS + h:DN_HEADS + h + 1, :]), 0.0)
                 for g, (p, h) in zip(gcc, units)]
        eg = [jnp.exp(g) for g in gcc]
        gend = [jnp.where(rcol < CHUNK, g[CHUNK - 1:CHUNK, :], g[PAIR - 1:PAIR, :]) for g in gcc]
        kb = [x.astype(BF16) for x in k]
        yield
        qkk = [lax.dot_general(jnp.concatenate([qq.astype(BF16), kk], axis=0), kk,
                               (((1,), (1,)), ((), ())), preferred_element_type=F32) for qq, kk in zip(q, kb)]
        yield
        ms = [jnp.where(strict, b * x[PAIR:] * d, 0.0) for b, x, d in zip(beta, qkk, decay)]
        tinv = yield from _tri_inverse(ms, row, col)
        sol = [jnp.dot(t.astype(BF16),
                       jnp.concatenate([(b * vv).astype(BF16), (b * e * kk).astype(BF16)], axis=1),
                       preferred_element_type=F32) for t, b, vv, e, kk in zip(tinv, beta, v, eg, k)]
        yield
        solb = [x.astype(BF16) for x in sol]
        qkb = [jnp.where(lower, x[:PAIR] * d, 0.0).astype(BF16) for x, d in zip(qkk, decay)]
        qwu = [jnp.dot(a, b, preferred_element_type=F32) for a, b in zip(qkb, solb)]
        yield
        kd = [kk * jnp.exp(ge - g) for kk, ge, g in zip(k, gend, gcc)]
        for c in range(PAIR // CHUNK):
            kdc = [jnp.where((row >> 6) == c, x, 0.0).astype(BF16) for x in kd]
            kwu = [lax.dot_general(a, b, (((0,), (0,)), ((), ())), preferred_element_type=F32)
                   for a, b in zip(kdc, solb)]
            yield
            for (p, h), x, g in zip(units, kwu, gcc):
                cc = p * (PAIR // CHUNK) + c
                bc_ref[h, cc] = x[:, :DN_DV]
                mc_ref[h, cc] = x[:, DN_DV:].astype(BF16)
                ge_ref[h, cc] = jnp.broadcast_to(jnp.exp(g[(c + 1) * CHUNK - 1:(c + 1) * CHUNK, :]), (8, LANES))
        for (p, h), rs, x, qq, e in zip(units, rows_of, qwu, q, eg):
            ol_ref[h, rs, :] = x[:, :DN_DV]
            qt_ref[h, rs, :] = (qq * e - x[:, DN_DV:]).astype(BF16)
        yield

    gain = gain_ref[...]
    st = [state_ref[h] for h in heads]

    def recurrence(w):
        for cc in range(w * (DN_WAVE // CHUNK), (w + 1) * (DN_WAVE // CHUNK)):
            cr = slice(cc * CHUNK, (cc + 1) * CHUNK)
            sb = [x.astype(BF16) for x in st]
            o = [ol_ref[h, cr, :] + jnp.dot(qt_ref[h, cr, :], sb[h], preferred_element_type=F32) for h in heads]
            new = [st[h] * ge_ref[h, cc, 0:1, :] + bc_ref[h, cc]
                   - jnp.dot(mc_ref[h, cc], sb[h], preferred_element_type=F32) for h in heads]
            for h in heads:
                st[h] = new[h]
            yield
            for h in heads:
                hs = slice(h * DN_DV, (h + 1) * DN_DV)
                on = o[h] * lax.rsqrt(jnp.mean(o[h] * o[h], axis=-1, keepdims=True) + EPS) * gain
                zz = z_ref[cr, hs].astype(F32)
                o_ref[cr, hs] = (on * _silu(zz)).astype(o_ref.dtype)
            yield

    _round_robin(activations(0))
    for w in range(nwave):
        others = []
        if w + 1 < nwave:
            others.append(activations(w + 1))
        if w > 0:
            others.append(recurrence(w - 1))
        _round_robin(chunk_local(w), *others)
    _round_robin(recurrence(nwave - 1))
    for h in heads:
        state_ref[h] = st[h]
    for s in range(nslab):
        buf_ref[s, 0:8, :] = buf_ref[s, ts:ts + 8, :]


def _deltanet(proj, ba, conv_w, par, gain, ltri3, batch, seq):
    nsb = seq // DN_TS
    nchunk = DN_TS // CHUNK
    return pl.pallas_call(
        _deltanet_kernel,
        grid=(batch, nsb),
        in_specs=[
            pl.BlockSpec((DN_TS, A_QKV), lambda b, s: (b * nsb + s, 0)),
            pl.BlockSpec((DN_TS, DN_WIDTH), lambda b, s: (b * nsb + s, A_QKV // DN_WIDTH)),
            pl.BlockSpec((DN_TS, LANES), lambda b, s: (b * nsb + s, 0)),
            pl.BlockSpec((DN_CONV, A_QKV), lambda b, s: (0, 0)),
            pl.BlockSpec((8, LANES), lambda b, s: (0, 0)),
            pl.BlockSpec((1, DN_DV), lambda b, s: (0, 0)),
            pl.BlockSpec((PAIR, 3 * PAIR), lambda b, s: (0, 0)),
        ],
        out_specs=pl.BlockSpec((DN_TS, DN_WIDTH), lambda b, s: (b * nsb + s, 0)),
        out_shape=jax.ShapeDtypeStruct((batch * seq, DN_WIDTH), BF16),
        scratch_shapes=[
            pltpu.VMEM((A_QKV // LANES, DN_TS + 8, LANES), F32),
            pltpu.VMEM((DN_TS, A_QKV), F32),
            pltpu.VMEM((DN_HEADS, DN_DK, DN_DV), F32),
            pltpu.VMEM((DN_HEADS, DN_TS, DN_DV), F32),
            pltpu.VMEM((DN_HEADS, DN_TS, DN_DK), BF16),
            pltpu.VMEM((DN_HEADS, nchunk, DN_DK, DN_DK), BF16),
            pltpu.VMEM((DN_HEADS, nchunk, DN_DK, DN_DV), F32),
            pltpu.VMEM((DN_HEADS, nchunk, 8, LANES), F32),
        ],
        compiler_params=pltpu.CompilerParams(
            dimension_semantics=("arbitrary", "arbitrary"), vmem_limit_bytes=VMEM_LIMIT),
        name="deltanet",
    )(proj, proj, ba, conv_w, par, gain, ltri3)


def _rms_halves(x, half_ones2, gain):
    hi, lo = _split2(x * x)
    ss = jnp.dot(jnp.concatenate([hi, lo], axis=1), half_ones2, preferred_element_type=F32)
    return x * lax.rsqrt(ss + SB_DH * EPS) * (gain * SB_DH ** 0.5)


def _sb_block(i, k_ref, cum_ref, o_ref, q2_ref, kn_ref, v2_ref, r_ref, acc_ref, flag_ref):
    npair = k_ref.shape[1] // LANES
    pairs = range(npair)

    def cols(p):
        return slice(p * LANES, (p + 1) * LANES)

    q2 = [q2_ref[p, i] for p in pairs]

    def scores(rows, mask):
        zs = [lax.dot_general(q2[p], kn_ref[rows, cols(p)], (((1,), (1,)), ((), ())),
                              preferred_element_type=F32) for p in pairs]
        lss = [jnp.minimum(z, 0.0) - jnp.log(1.0 + jnp.exp2(-jnp.abs(z))) * LOG2E for z in zs]
        lks = [ls - z for ls, z in zip(lss, zs)]
        if mask is not None:
            lks = [jnp.where(mask, lk, 0.0) for lk in lks]
        return lss, [jnp.concatenate(_split2(lk), axis=1) for lk in lks]

    def cumsums(lk2s):
        return [jnp.dot(lk2, cum_ref[...], preferred_element_type=F32) for lk2 in lk2s]

    def weights(lss, css, rs, mask):
        if rs is None:
            avs = [jnp.exp2(ls + cs[:, :SB_BLOCK]) for ls, cs in zip(lss, css)]
        else:
            avs = [jnp.exp2(ls + cs[:, :SB_BLOCK] + r) for ls, cs, r in zip(lss, css, rs)]
        if mask is not None:
            avs = [jnp.where(mask, a, 0.0) for a in avs]
        out = []
        for a in avs:
            ab = a.astype(BF16)
            out.append(jnp.concatenate([ab[:SB_BLOCK], ab[SB_BLOCK:]], axis=1))
        return out

    def tile_rows(j):
        return pl.ds(pl.multiple_of(j * SB_BLOCK, SB_BLOCK), SB_BLOCK)

    def sweep(tiles, masks, rs, accs):
        sc = [scores(tile_rows(j), m) for j, m in zip(tiles, masks)]
        css = [cumsums(lk2s) for _, lk2s in sc]
        for j, m, (lss, _), cs in zip(tiles, masks, sc, css):
            a2s = weights(lss, cs, rs, m)
            rs = [c[:, SB_BLOCK:] for c in cs] if rs is None else [r + c[:, SB_BLOCK:] for r, c in zip(rs, cs)]
            accs = [acc + jnp.dot(a2, v2_ref[p, j], preferred_element_type=F32)
                    for p, (acc, a2) in enumerate(zip(accs, a2s))]
        return rs, accs

    def load_state():
        return [r_ref[p] for p in pairs], [acc_ref[:, cols(p)] for p in pairs]

    def store_state(rs, accs):
        for p in pairs:
            r_ref[p] = rs[p]
            acc_ref[:, cols(p)] = accs[p]

    qrow = lax.broadcasted_iota(jnp.int32, (2 * SB_BLOCK, SB_BLOCK), 0) & (SB_BLOCK - 1)
    kcol = lax.broadcasted_iota(jnp.int32, (2 * SB_BLOCK, SB_BLOCK), 1)
    causal = kcol < qrow
    zero_accs = [jnp.zeros((SB_BLOCK, LANES), F32)] * npair

    def alive_flag(rs):
        return (jnp.max(functools.reduce(jnp.maximum, rs)) >= ZERO_WEIGHT_LOG2).astype(jnp.int32)

    def head_rows(x, lo, hi):
        return jnp.concatenate([x[lo:hi], x[SB_BLOCK + lo:SB_BLOCK + hi]], axis=0)

    @pl.when(i == 0)
    def _():
        rs, accs = sweep([i], [causal], None, zero_accs)
        store_state(rs, accs)
        flag_ref[0] = alive_flag(rs)
        flag_ref[1] = jnp.int32(0)

    @pl.when(i == 1)
    def _():
        rs, accs = sweep([i, i - 1], [causal, None], None, zero_accs)
        store_state(rs, accs)
        flag_ref[0] = alive_flag(rs)
        flag_ref[1] = jnp.int32(0)

    @pl.when(i >= 2)
    def _():
        j3 = i - 2
        sc = [scores(tile_rows(j), m) for j, m in ((i, causal), (i - 1, None))]
        q2n = [head_rows(q, 0, SB_NARROW) for q in q2]
        zs = [lax.dot_general(q2n[p], kn_ref[tile_rows(j3), cols(p)], (((1,), (1,)), ((), ())),
                              preferred_element_type=F32) for p in pairs]
        lss_n = [jnp.minimum(z, 0.0) - jnp.log(1.0 + jnp.exp2(-jnp.abs(z))) * LOG2E for z in zs]
        lk2_n = [jnp.concatenate(_split2(ls - z), axis=1) for ls, z in zip(lss_n, zs)]
        css = [cumsums(lk2s) for _, lk2s in sc]
        css_n = cumsums(lk2_n)
        rs, accs = None, zero_accs
        for j, m, (lss, _), cs in zip((i, i - 1), (causal, None), sc, css):
            a2s = weights(lss, cs, rs, m)
            rs = [c[:, SB_BLOCK:] for c in cs] if rs is None else [r + c[:, SB_BLOCK:] for r, c in zip(rs, cs)]
            accs = [acc + jnp.dot(a2, v2_ref[p, j], preferred_element_type=F32)
                    for p, (acc, a2) in enumerate(zip(accs, a2s))]
        rest = [head_rows(r, SB_NARROW, SB_BLOCK) for r in rs]
        rs_n = [head_rows(r, 0, SB_NARROW) for r in rs]
        a_n = [jnp.exp2(ls + cs[:, :SB_BLOCK] + r).astype(BF16) for ls, cs, r in zip(lss_n, css_n, rs_n)]
        top = [accs[p][0:SB_NARROW]
               + jnp.dot(jnp.concatenate([a_n[p][:SB_NARROW], a_n[p][SB_NARROW:]], axis=1), v2_ref[p, j3],
                         preferred_element_type=F32) for p in pairs]
        rs_n = [r + cs[:, SB_BLOCK:] for r, cs in zip(rs_n, css_n)]
        store_state(rs, accs)
        for p in pairs:
            r_ref[p, 0:SB_NARROW, :] = rs_n[p][:SB_NARROW]
            r_ref[p, SB_BLOCK:SB_BLOCK + SB_NARROW, :] = rs_n[p][SB_NARROW:]
            acc_ref[0:SB_NARROW, cols(p)] = top[p]
        flag_ref[1] = alive_flag(rest)
        flag_ref[0] = jnp.maximum(alive_flag(rest), alive_flag(rs_n))

    @pl.when(flag_ref[1] > 0)
    def _():
        rs, accs = sweep([i - 2], [qrow >= SB_NARROW], *load_state())
        store_state(rs, accs)
        flag_ref[0] = alive_flag(rs)

    def cond(carry):
        t, alive = carry
        return jnp.logical_and(t < i, alive > 0)

    def body(carry):
        t, _ = carry
        j = i - 1 - t
        rs, accs = sweep([j], [None], *load_state())
        store_state(rs, accs)
        return t + 1, alive_flag(rs)
    lax.while_loop(cond, body, (jnp.int32(SB_LEAD - 1), flag_ref[0]))

    o_ref[...] = acc_ref[...].astype(o_ref.dtype)


def _sb_kernel(q_ref, k_ref, v_ref, qg_ref, kg_ref, cum_ref, half_ref, o_ref,
               q2_ref, kn_ref, v2_ref, r_ref, acc_ref, flag_ref):
    step = pl.program_id(1)
    nblk = k_ref.shape[0] // SB_BLOCK
    pairs = range(k_ref.shape[1] // LANES)
    first = lax.broadcasted_iota(jnp.int32, (SB_BLOCK, LANES), 1) < SB_DH

    def cols(p):
        return slice(p * LANES, (p + 1) * LANES)

    @pl.when(step == 0)
    def _():
        qscale = qg_ref[...] * (SB_DH ** -0.5 * LOG2E)

        def prep(j, carry):
            rows = pl.ds(pl.multiple_of(j * SB_BLOCK, SB_BLOCK), SB_BLOCK)
            for p in pairs:
                qn = _rms_halves(q_ref[rows, cols(p)].astype(F32), half_ref[...], qscale).astype(BF16)
                zero = jnp.zeros_like(qn)
                q2_ref[p, j, 0:SB_BLOCK, :] = jnp.where(first, qn, zero)
                q2_ref[p, j, SB_BLOCK:2 * SB_BLOCK, :] = jnp.where(first, zero, qn)
                kn = _rms_halves(k_ref[rows, cols(p)].astype(F32), half_ref[...], kg_ref[...])
                kn_ref[rows, cols(p)] = kn.astype(BF16)
                vb = v_ref[rows, cols(p)]
                v2_ref[p, j, 0:SB_BLOCK, :] = jnp.where(first, vb, zero)
                v2_ref[p, j, SB_BLOCK:2 * SB_BLOCK, :] = jnp.where(first, zero, vb)
            return carry
        lax.fori_loop(0, nblk, prep, 0)

    def block(sub, carry):
        out_rows = pl.ds(pl.multiple_of(sub * SB_BLOCK, SB_BLOCK), SB_BLOCK)
        _sb_block(step * SB_QB + sub, k_ref, cum_ref, o_ref.at[out_rows, :],
                  q2_ref, kn_ref, v2_ref, r_ref, acc_ref, flag_ref)
        return carry
    lax.fori_loop(0, SB_QB, block, 0)


def _stick_breaking(proj, qg2, kg2, cum_mat, half_ones2, batch, seq):
    nq = seq // SB_BLOCK
    npair = SB_HEADS // 2
    qcol = (A_QKV + DN_WIDTH) // SB_WIDTH
    return pl.pallas_call(
        _sb_kernel,
        grid=(batch, nq // SB_QB),
        in_specs=[
            pl.BlockSpec((seq, SB_WIDTH), lambda b, i: (b, qcol)),
            pl.BlockSpec((seq, SB_WIDTH), lambda b, i: (b, qcol + 1)),
            pl.BlockSpec((seq, SB_WIDTH), lambda b, i: (b, qcol + 2)),
            pl.BlockSpec((1, LANES), lambda b, i: (0, 0)),
            pl.BlockSpec((1, LANES), lambda b, i: (0, 0)),
            pl.BlockSpec((2 * SB_BLOCK, 2 * SB_BLOCK), lambda b, i: (0, 0)),
            pl.BlockSpec((2 * LANES, LANES), lambda b, i: (0, 0)),
        ],
        out_specs=pl.BlockSpec((SB_QB * SB_BLOCK, SB_WIDTH), lambda b, i: (b * (nq // SB_QB) + i, 0)),
        out_shape=jax.ShapeDtypeStruct((batch * seq, SB_WIDTH), BF16),
        scratch_shapes=[
            pltpu.VMEM((npair, nq, 2 * SB_BLOCK, LANES), BF16),
            pltpu.VMEM((seq, SB_WIDTH), BF16),
            pltpu.VMEM((npair, nq, 2 * SB_BLOCK, LANES), BF16),
            pltpu.VMEM((npair, 2 * SB_BLOCK, SB_BLOCK), F32),
            pltpu.VMEM((SB_BLOCK, SB_WIDTH), F32),
            pltpu.SMEM((2,), jnp.int32),
        ],
        compiler_params=pltpu.CompilerParams(
            dimension_semantics=("arbitrary", "arbitrary"), vmem_limit_bytes=VMEM_LIMIT),
        name="stickbreak",
    )(proj, proj, proj, qg2, kg2, cum_mat, half_ones2)


def _merge_kernel(x_ref, gain_ref, oa_ref, ob_ref, zb_ref, wg_ref, bg_ref, wa_ref, wb_ref, wo_ref, out_ref):
    x = x_ref[...]
    xn = (x * lax.rsqrt(jnp.mean(x * x, axis=-1, keepdims=True) + EPS) * gain_ref[...]).astype(BF16)
    gates = jax.nn.sigmoid(jnp.dot(xn, wg_ref[...], preferred_element_type=F32) + bg_ref[...])
    zb = zb_ref[...].astype(F32)
    ob = (ob_ref[...].astype(F32) * _silu(zb)).astype(BF16)
    ya = jnp.dot(oa_ref[...], wa_ref[...], preferred_element_type=F32)
    yb = jnp.dot(ob, wb_ref[...], preferred_element_type=F32)
    merged = gates[:, :D_MODEL] * ya + gates[:, D_MODEL:] * yb
    out_ref[...] = x + jnp.dot(merged.astype(BF16), wo_ref[...], preferred_element_type=F32)


def _merge(x2, gain, o_a, o_b, proj, w_gate, b_gate, w_up_a, w_up_b, w_out):
    t = x2.shape[0]
    zb_col = (A_QKV + DN_WIDTH + 3 * SB_WIDTH) // SB_WIDTH
    const = lambda i: (0, 0)
    return pl.pallas_call(
        _merge_kernel,
        grid=(t // OUT_TM,),
        in_specs=[
            pl.BlockSpec((OUT_TM, D_MODEL), lambda i: (i, 0)),
            pl.BlockSpec((1, D_MODEL), const),
            pl.BlockSpec((OUT_TM, DN_WIDTH), lambda i: (i, 0)),
            pl.BlockSpec((OUT_TM, SB_WIDTH), lambda i: (i, 0)),
            pl.BlockSpec((OUT_TM, SB_WIDTH), lambda i: (i, zb_col)),
            pl.BlockSpec((D_MODEL, 2 * D_MODEL), const, pipeline_mode=pl.Buffered(1)),
            pl.BlockSpec((1, 2 * D_MODEL), const),
            pl.BlockSpec((DN_WIDTH, D_MODEL), const, pipeline_mode=pl.Buffered(1)),
            pl.BlockSpec((SB_WIDTH, D_MODEL), const, pipeline_mode=pl.Buffered(1)),
            pl.BlockSpec((D_MODEL, D_MODEL), const, pipeline_mode=pl.Buffered(1)),
        ],
        out_specs=pl.BlockSpec((OUT_TM, D_MODEL), lambda i: (i, 0)),
        out_shape=jax.ShapeDtypeStruct((t, D_MODEL), F32),
        compiler_params=pltpu.CompilerParams(
            dimension_semantics=("arbitrary",), vmem_limit_bytes=VMEM_LIMIT),
        name="merge",
    )(x2, gain, o_a, o_b, proj, w_gate, b_gate, w_up_a, w_up_b, w_out)


def _layer(x, norm_gain, w_in, b_gate, conv_w, a_log, dt_bias, dn_out_gain,
           sb_q_gain, sb_k_gain, w_up_a, w_up_b, w_out):
    batch, seq, _ = x.shape
    x2 = x.reshape(batch * seq, D_MODEL)

    w_main = jnp.concatenate([w_in[:, :BA_COL0], w_in[:, BA_COL0 + BA_COLS:GATE_COL0]], axis=1).astype(BF16)
    w_gate = w_in[:, GATE_COL0:].astype(BF16)
    w_ba = jnp.pad(w_in[:, BA_COL0:BA_COL0 + BA_COLS], ((0, 0), (0, LANES - BA_COLS)))
    wba_hi = w_ba.astype(BF16)
    wba_lo = (w_ba - wba_hi.astype(F32)).astype(BF16)
    par = jnp.zeros((8, LANES), F32)
    par = par.at[0, DN_HEADS:2 * DN_HEADS].set(a_log).at[1, DN_HEADS:2 * DN_HEADS].set(dt_bias)
    qg2 = jnp.concatenate([sb_q_gain, sb_q_gain])[None, :]
    kg2 = jnp.concatenate([sb_k_gain, sb_k_gain])[None, :]
    kk = jnp.arange(SB_BLOCK)
    cum1 = jnp.concatenate([(kk[:, None] > kk[None, :]).astype(BF16), jnp.ones((SB_BLOCK, SB_BLOCK), BF16)], axis=1)
    cum_mat = jnp.concatenate([cum1, cum1], axis=0)
    ln = jnp.arange(LANES)
    half1 = ((ln[:, None] // SB_DH) == (ln[None, :] // SB_DH)).astype(BF16)
    half_ones2 = jnp.concatenate([half1, half1], axis=0)
    tt = jnp.arange(PAIR)
    ltri1 = ((tt[:, None] // CHUNK == tt[None, :] // CHUNK) & (tt[:, None] >= tt[None, :])).astype(BF16)
    ltri3 = jnp.concatenate([ltri1, ltri1, ltri1], axis=1)

    proj, ba = _inproj(x2, norm_gain[None, :], w_main, wba_hi, wba_lo)
    o_a = _deltanet(proj, ba, conv_w, par, dn_out_gain[None, :], ltri3, batch, seq)
    o_b = _stick_breaking(proj, qg2, kg2, cum_mat, half_ones2, batch, seq)
    out = _merge(x2, norm_gain[None, :], o_a, o_b, proj, w_gate, b_gate[None, :], w_up_a.astype(BF16),
                 w_up_b.astype(BF16), w_out.astype(BF16))
    return out.reshape(batch, seq, D_MODEL)


def kernel(x, norm_gain, w_in, b_gate, conv_w, a_log, dt_bias, dn_out_gain, sb_q_gain, sb_k_gain,
           w_up_a, w_up_b, w_out):
    h = x
    for layer in range(norm_gain.shape[0]):
        h = _layer(h, norm_gain[layer], w_in[layer], b_gate[layer], conv_w[layer], a_log[layer],
                   dt_bias[layer], dn_out_gain[layer], sb_q_gain[layer], sb_k_gain[layer],
                   w_up_a[layer], w_up_b[layer], w_out[layer])
    return h
```
